```python
import math
import jax, jax.numpy as jnp
from jax import lax
import numpy as np

D_MODEL = 1024
BATCH = 16
SEQ = 2048
DEPTH = 2
DEC_BATCH = 128
DEC_SEQ = 8
PAST_LEN = 16384
PAGE_SIZE = 128

D_CHUNK = 512
CHUNK = 128
N_CHUNK_GROUPS = 4
CHUNK_GROUP_W = D_CHUNK // N_CHUNK_GROUPS
D_LRU = 512
N_LRU_BLOCKS = 8
LRU_BLOCK = D_LRU // N_LRU_BLOCKS
CONV_W = 4
LRU_C = 8.0
N_HEADS = 8
N_KV_HEADS = 2
N_GROUP = N_HEADS // N_KV_HEADS
HEAD_DIM = 64
D_ATTN = N_HEADS * HEAD_DIM
D_KV = N_KV_HEADS * HEAD_DIM
WINDOW = 128
N_BRANCH = 3
D_FF = 3584
N_EXPERTS = 8
TOP_K = 2
N_DENSE = (DEPTH + 1) // 2
N_MOE = DEPTH // 2
EPS = 1e-6

SPLIT_UV = 2 * D_CHUNK
SPLIT_LX = SPLIT_UV + D_LRU
SPLIT_LG = SPLIT_LX + D_LRU
SPLIT_Q = SPLIT_LG + D_ATTN
SPLIT_K = SPLIT_Q + D_KV
SPLIT_V = SPLIT_K + D_KV
D_IN = SPLIT_V + N_BRANCH * D_MODEL

kernel_name = "hybrid_gmlp_rglru_swa_decoder_step"


def rms_norm(x, g):
    xf = x.astype(jnp.float32)
    y = xf * lax.rsqrt(jnp.mean(xf * xf, axis=-1, keepdims=True) + EPS)
    return (y * g.astype(jnp.float32)).astype(x.dtype)


def layer_norm(x, g, b):
    xf = x.astype(jnp.float32)
    mu = jnp.mean(xf, axis=-1, keepdims=True)
    xc = xf - mu
    y = xc * lax.rsqrt(jnp.mean(xc * xc, axis=-1, keepdims=True) + EPS)
    return (y * g.astype(jnp.float32) + b.astype(jnp.float32)).astype(x.dtype)


def chunk_mlp(uv, w_s, b_s, ln_g, ln_b):
    bsz, t, _ = uv.shape
    uv = jax.nn.gelu(uv)
    u, v = jnp.split(uv, 2, axis=-1)
    v = layer_norm(v, ln_g, ln_b)
    n_chunks = -(-t // CHUNK)
    pad = n_chunks * CHUNK - t
    vp = jnp.pad(v, ((0, 0), (0, pad), (0, 0))).reshape(bsz, n_chunks, CHUNK, N_CHUNK_GROUPS, CHUNK_GROUP_W)
    causal = jnp.tril(jnp.ones((CHUNK, CHUNK), dtype=bool))
    w = jnp.where(causal[None], w_s, 0.0).astype(v.dtype)
    s = jnp.einsum('gts,bcsgd->bctgd', w, vp) + b_s.T.astype(v.dtype)[None, None, :, :, None]
    s = s.reshape(bsz, n_chunks * CHUNK, D_CHUNK)[:, :t]
    return u * s, v


def causal_conv(x, buf, w, b):
    t = x.shape[1]
    xp = jnp.concatenate([buf.astype(x.dtype), x], axis=1)
    out = b + w[0] * xp[:, 0:t]
    for j in range(1, CONV_W):
        out = out + w[j] * xp[:, j:j + t]
    return out, xp[:, -(CONV_W - 1):]


def rg_lru(x, h0, lam, w_a, b_a, w_i, b_i):
    bsz, t, _ = x.shape
    xb = x.reshape(bsz, t, N_LRU_BLOCKS, LRU_BLOCK)
    r = jax.nn.sigmoid(jnp.einsum('btnd,nde->btne', xb, w_a).reshape(bsz, t, D_LRU) + b_a)
    i = jax.nn.sigmoid(jnp.einsum('btnd,nde->btne', xb, w_i).reshape(bsz, t, D_LRU) + b_i)
    log_a = LRU_C * r.astype(jnp.float32) * jax.nn.log_sigmoid(lam.astype(jnp.float32))
    a = jnp.exp(log_a)
    inp = jnp.sqrt(-jnp.expm1(2.0 * log_a)) * (i * x).astype(jnp.float32)

    def step(h, ab):
        a_t, b_t = ab
        h = a_t * h + b_t
        return h, h

    h_last, hs = lax.scan(step, h0.astype(jnp.float32), (jnp.swapaxes(a, 0, 1), jnp.swapaxes(inp, 0, 1)))
    return jnp.swapaxes(hs, 0, 1).astype(x.dtype), h_last.astype(x.dtype)


def band_blocks(k, nb):
    bsz, t = k.shape[0], k.shape[1]
    kp = jnp.pad(k, ((0, 0), (WINDOW, 0), (0, 0), (0, 0)))
    prev = kp[:, :t].reshape(bsz, nb, WINDOW, N_KV_HEADS, HEAD_DIM)
    cur = k.reshape(bsz, nb, WINDOW, N_KV_HEADS, HEAD_DIM)
    return jnp.concatenate([prev, cur], axis=2)


def window_attention(q, k, v, key_valid, sinks):
    bsz, nb, nq = q.shape[0], q.shape[1], q.shape[2]
    nk = k.shape[2]
    qg = q.reshape(bsz, nb, nq, N_KV_HEADS, N_GROUP, HEAD_DIM)
    logits = jnp.einsum('bnqkgd,bnskd->bnkgqs', qg, k, preferred_element_type=jnp.float32) * (HEAD_DIM ** -0.5)
    dist = (WINDOW + jnp.arange(nq)[:, None] - jnp.arange(nk)[None, :]).astype(jnp.float32)
    allowed = ((dist >= 0) & (dist <= WINDOW))[None] & key_valid[:, None, :]
    slopes = jnp.exp2(-8.0 * (jnp.arange(N_HEADS, dtype=jnp.float32) + 1.0) / N_HEADS)
    logits = logits - slopes.reshape(N_KV_HEADS, N_GROUP)[:, :, None, None] * dist
    logits = jnp.where(allowed[None, :, None, None], logits, -jnp.inf)
    sink = jnp.broadcast_to(sinks.astype(jnp.float32).reshape(N_KV_HEADS, N_GROUP)[:, :, None, None],
                            logits.shape[:-1] + (1,))
    probs = jax.nn.softmax(jnp.concatenate([logits, sink], axis=-1), axis=-1)[..., :-1]
    out = jnp.einsum('bnkgqs,bnskd->bnqkgd', probs.astype(v.dtype), v)
    return out.reshape(bsz, nb, nq, D_ATTN)


def hybrid_mixer(x, lp, conv_buf, h0, win_k, win_v):
    bsz, t, _ = x.shape
    xn = rms_norm(x, lp['norm1_g'])
    proj = xn @ lp['w_in']
    uv, lru_x, lru_gate, q, k, v, gate_logits = jnp.split(
        proj, [SPLIT_UV, SPLIT_LX, SPLIT_LG, SPLIT_Q, SPLIT_K, SPLIT_V], axis=-1)
    a_out, chunk_v = chunk_mlp(uv, lp['chunk_ws'], lp['chunk_bs'], lp['chunk_ln_g'], lp['chunk_ln_b'])
    xc, conv_new = causal_conv(lru_x, conv_buf, lp['conv_w'], lp['conv_b'])
    hs, h_new = rg_lru(xc, h0, lp['lru_lambda'], lp['lru_wa'], lp['lru_ba'], lp['lru_wi'], lp['lru_bi'])
    b_out = jax.nn.gelu(lru_gate) * hs
    q = rms_norm(q.reshape(bsz, t, N_HEADS, HEAD_DIM), lp['q_norm_g'])
    k = rms_norm(k.reshape(bsz, t, N_KV_HEADS, HEAD_DIM), lp['k_norm_g'])
    v = v.reshape(bsz, t, N_KV_HEADS, HEAD_DIM)
    if win_k is None:
        nb = t // WINDOW
        qb = q.reshape(bsz, nb, WINDOW, N_HEADS, HEAD_DIM)
        kb = band_blocks(k, nb)
        vb = band_blocks(v, nb)
        key_pos = (jnp.arange(nb)[:, None] - 1) * WINDOW + jnp.arange(2 * WINDOW)[None, :]
        key_valid = key_pos >= 0
        k_keep, v_keep = k[:, -WINDOW:], v[:, -WINDOW:]
    else:
        kc = jnp.concatenate([win_k.astype(k.dtype), k], axis=1)
        vc = jnp.concatenate([win_v.astype(v.dtype), v], axis=1)
        qb, kb, vb = q[:, None], kc[:, None], vc[:, None]
        key_valid = jnp.ones((1, WINDOW + t), dtype=bool)
        k_keep, v_keep = kc[:, -WINDOW:], vc[:, -WINDOW:]
    c_out = window_attention(qb, kb, vb, key_valid, lp['attn_sinks']).reshape(bsz, t, D_ATTN)
    gates = jax.nn.sigmoid(gate_logits).reshape(bsz, t, N_BRANCH, D_MODEL)
    merged = (gates[:, :, 0] * (a_out @ lp['w_proj_a'])
              + gates[:, :, 1] * (b_out @ lp['w_proj_b'])
              + gates[:, :, 2] * (c_out @ lp['w_proj_c']))
    y = x + merged @ lp['w_out']
    return y, (k_keep, v_keep, conv_new, h_new, chunk_v)


def swiglu(x, w_gate, w_up, w_down):
    return (jax.nn.silu(x @ w_gate) * (x @ w_up)) @ w_down


def moe_ffn(x, w_router, w_gate, w_up, w_down):
    logits = jnp.einsum('btd,de->bte', x, w_router, preferred_element_type=jnp.float32)
    top_vals, top_idx = lax.top_k(logits, TOP_K)
    top_w = jax.nn.softmax(top_vals, axis=-1)
    combine = jnp.sum(jax.nn.one_hot(top_idx, N_EXPERTS, dtype=jnp.float32) * top_w[..., None], axis=-2)
    combine = combine.astype(x.dtype)
    out = jnp.zeros_like(x)
    for e in range(N_EXPERTS):
        out = out + combine[..., e:e + 1] * swiglu(x, w_gate[e], w_up[e], w_down[e])
    return out


def channel_mixer(x, l, norm2_g, ffn_w_gate, ffn_w_up, ffn_w_down, moe_router, moe_w_gate, moe_w_up, moe_w_down):
    xn = rms_norm(x, norm2_g[l])
    if l % 2 == 0:
        j = l // 2
        out = swiglu(xn, ffn_w_gate[j], ffn_w_up[j], ffn_w_down[j])
    else:
        j = l // 2
        out = moe_ffn(xn, moe_router[j], moe_w_gate[j], moe_w_up[j], moe_w_down[j])
    return x + out


def setup_inputs(seed: int = 0) -> dict:
    key = jax.random.key(seed)
    keys = list(jax.random.split(key, 48))

    def nrm(shape, scale=1.0):
        return scale * jax.random.normal(keys.pop(), shape, jnp.float32)

    a_base = jax.random.uniform(keys.pop(), (DEPTH, D_LRU), jnp.float32, 0.9, 0.999)
    a_root = a_base ** (1.0 / LRU_C)
    lru_lambda = jnp.log(a_root) - jnp.log1p(-a_root)
    return {
        'x_prompt': nrm((BATCH, SEQ, D_MODEL)),
        'x_sample': nrm((DEC_BATCH, DEC_SEQ, D_MODEL)),
        'cache_win_k': nrm((DEPTH, DEC_BATCH, WINDOW, N_KV_HEADS, HEAD_DIM)),
        'cache_win_v': nrm((DEPTH, DEC_BATCH, WINDOW, N_KV_HEADS, HEAD_DIM)),
        'state_conv': nrm((DEPTH, DEC_BATCH, CONV_W - 1, D_LRU)),
        'state_lru_h': nrm((DEPTH, DEC_BATCH, D_LRU)),
        'norm1_g': 1.0 + nrm((DEPTH, D_MODEL), 0.02),
        'w_in': nrm((DEPTH, D_MODEL, D_IN), D_MODEL ** -0.5),
        'chunk_ln_g': 1.0 + nrm((DEPTH, D_CHUNK), 0.02),
        'chunk_ln_b': nrm((DEPTH, D_CHUNK), 0.02),
        'chunk_ws': nrm((DEPTH, N_CHUNK_GROUPS, CHUNK, CHUNK), 0.5 * CHUNK ** -0.5),
        'chunk_bs': 1.0 + nrm((DEPTH, N_CHUNK_GROUPS, CHUNK), 0.1),
        'conv_w': nrm((DEPTH, CONV_W, D_LRU), CONV_W ** -0.5),
        'conv_b': nrm((DEPTH, D_LRU), 0.02),
        'lru_lambda': lru_lambda,
        'lru_wa': nrm((DEPTH, N_LRU_BLOCKS, LRU_BLOCK, LRU_BLOCK), LRU_BLOCK ** -0.5),
        'lru_ba': nrm((DEPTH, D_LRU), 0.02),
        'lru_wi': nrm((DEPTH, N_LRU_BLOCKS, LRU_BLOCK, LRU_BLOCK), LRU_BLOCK ** -0.5),
        'lru_bi': nrm((DEPTH, D_LRU), 0.02),
        'q_norm_g': 1.0 + nrm((DEPTH, HEAD_DIM), 0.02),
        'k_norm_g': 1.0 + nrm((DEPTH, HEAD_DIM), 0.02),
        'attn_sinks': nrm((DEPTH, N_HEADS), 0.5),
        'w_proj_a': nrm((DEPTH, D_CHUNK, D_MODEL), D_CHUNK ** -0.5),
        'w_proj_b': nrm((DEPTH, D_LRU, D_MODEL), D_LRU ** -0.5),
        'w_proj_c': nrm((DEPTH, D_ATTN, D_MODEL), D_ATTN ** -0.5),
        'w_out': nrm((DEPTH, D_MODEL, D_MODEL), D_MODEL ** -0.5),
        'norm2_g': 1.0 + nrm((DEPTH, D_MODEL), 0.02),
        'ffn_w_gate': nrm((N_DENSE, D_MODEL, D_FF), D_MODEL ** -0.5),
        'ffn_w_up': nrm((N_DENSE, D_MODEL, D_FF), D_MODEL ** -0.5),
        'ffn_w_down': nrm((N_DENSE, D_FF, D_MODEL), D_FF ** -0.5),
        'moe_router': nrm((N_MOE, D_MODEL, N_EXPERTS), D_MODEL ** -0.5),
        'moe_w_gate': nrm((N_MOE, N_EXPERTS, D_MODEL, D_FF), D_MODEL ** -0.5),
        'moe_w_up': nrm((N_MOE, N_EXPERTS, D_MODEL, D_FF), D_MODEL ** -0.5),
        'moe_w_down': nrm((N_MOE, N_EXPERTS, D_FF, D_MODEL), D_FF ** -0.5),
    }


def reference(x_prompt, x_sample, cache_win_k, cache_win_v, state_conv, state_lru_h,
              norm1_g, w_in, chunk_ln_g, chunk_ln_b, chunk_ws, chunk_bs,
              conv_w, conv_b, lru_lambda, lru_wa, lru_ba, lru_wi, lru_bi,
              q_norm_g, k_norm_g, attn_sinks, w_proj_a, w_proj_b, w_proj_c, w_out,
              norm2_g, ffn_w_gate, ffn_w_up, ffn_w_down,
              moe_router, moe_w_gate, moe_w_up, moe_w_down):
    yp, ys = x_prompt, x_sample
    pk, pv, pconv, ph = [], [], [], []
    sk, sv, sconv, sh, schunk = [], [], [], [], []
    for l in range(DEPTH):
        lp = {
            'norm1_g': norm1_g[l], 'w_in': w_in[l],
            'chunk_ln_g': chunk_ln_g[l], 'chunk_ln_b': chunk_ln_b[l],
            'chunk_ws': chunk_ws[l], 'chunk_bs': chunk_bs[l],
            'conv_w': conv_w[l], 'conv_b': conv_b[l], 'lru_lambda': lru_lambda[l],
            'lru_wa': lru_wa[l], 'lru_ba': lru_ba[l], 'lru_wi': lru_wi[l], 'lru_bi': lru_bi[l],
            'q_norm_g': q_norm_g[l], 'k_norm_g': k_norm_g[l], 'attn_sinks': attn_sinks[l],
            'w_proj_a': w_proj_a[l], 'w_proj_b': w_proj_b[l], 'w_proj_c': w_proj_c[l], 'w_out': w_out[l],
        }
        bp = yp.shape[0]
        conv0 = jnp.zeros((bp, CONV_W - 1, D_LRU), yp.dtype)
        h_zero = jnp.zeros((bp, D_LRU), jnp.float32)
        yp, st_p = hybrid_mixer(yp, lp, conv0, h_zero, None, None)
        ys, st_s = hybrid_mixer(ys, lp, state_conv[l], state_lru_h[l], cache_win_k[l], cache_win_v[l])
        yp = channel_mixer(yp, l, norm2_g, ffn_w_gate, ffn_w_up, ffn_w_down, moe_router, moe_w_gate, moe_w_up, moe_w_down)
        ys = channel_mixer(ys, l, norm2_g, ffn_w_gate, ffn_w_up, ffn_w_down, moe_router, moe_w_gate, moe_w_up, moe_w_down)
        pk.append(st_p[0]); pv.append(st_p[1]); pconv.append(st_p[2]); ph.append(st_p[3])
        sk.append(st_s[0]); sv.append(st_s[1]); sconv.append(st_s[2]); sh.append(st_s[3]); schunk.append(st_s[4])
    cache_win_k_prompt = jnp.stack(pk)
    cache_win_v_prompt = jnp.stack(pv)
    state_conv_prompt = jnp.stack(pconv)
    state_lru_h_prompt = jnp.stack(ph)
    cache_win_k_sample = jnp.stack(sk)
    cache_win_v_sample = jnp.stack(sv)
    state_conv_sample = jnp.stack(sconv)
    state_lru_h_sample = jnp.stack(sh)
    state_chunk_v_sample = jnp.stack(schunk)
    return (yp, ys, cache_win_k_prompt, cache_win_v_prompt, state_conv_prompt, state_lru_h_prompt,
            cache_win_k_sample, cache_win_v_sample, state_conv_sample, state_lru_h_sample, state_chunk_v_sample)
```

```python
import functools
import math

import jax
import jax.numpy as jnp
from jax import lax
from jax.experimental import pallas as pl
from jax.experimental.pallas import tpu as pltpu

F32 = jnp.float32
BF16 = jnp.bfloat16

D_MODEL = 1024
D_CHUNK = 512
CHUNK = 128
N_CHUNK_GROUPS = 4
D_LRU = 512
CONV_W = 4
LRU_C = 8.0
N_HEADS = 8
N_KV_HEADS = 2
HEAD_DIM = 64
D_ATTN = N_HEADS * HEAD_DIM
WINDOW = 128
D_FF = 3584
N_EXPERTS = 8
EPS = 1e-6
NEG = -1e30

O_UV = 0
O_LX = 2 * D_CHUNK
O_LG = O_LX + D_LRU
O_Q = O_LG + D_LRU
O_KK = O_Q + D_ATTN
O_VV = O_KK + 2 * N_KV_HEADS * HEAD_DIM
O_G = O_VV + 2 * N_KV_HEADS * HEAD_DIM
D_INP = O_G + 3 * D_MODEL

LANES = 128
MOE_SUB_TOKENS = 512
MOE_SUB_ROWS = 160
VMEM_LIMIT = 56 * 1024 * 1024


def _resident(shape):
    nd = len(shape)
    return pl.BlockSpec(shape, lambda *_: (0,) * nd, pipeline_mode=pl.Buffered(1))


def _mm(a, w):
    return jnp.dot(a.astype(BF16), w, preferred_element_type=F32)


def _gelu(x):
    return 0.5 * x * (1.0 + jnp.tanh(0.7978845608028654 * (x + 0.044715 * (x * x * x))))


def _rms(x, g):
    return x * lax.rsqrt(jnp.mean(x * x, axis=-1, keepdims=True) + EPS) * g


def _seg_scan(a, b, seg):
    pos = lax.broadcasted_iota(jnp.int32, (a.shape[0], 1), 0) & (seg - 1)
    s = 1
    while s < seg:
        ok = pos >= s
        a_sh = pltpu.roll(a, s, 0)
        b_sh = pltpu.roll(b, s, 0)
        b = jnp.where(ok, a * b_sh + b, b)
        a = jnp.where(ok, a * a_sh, a)
        s *= 2
    return a, b


def _chunk_branch(uv, wm, bs_full, ln_g, ln_b):
    uv = _gelu(uv)
    u = uv[:, :D_CHUNK]
    v = uv[:, D_CHUNK:]
    mu = jnp.mean(v, axis=-1, keepdims=True)
    vc = v - mu
    v = vc * lax.rsqrt(jnp.mean(vc * vc, axis=-1, keepdims=True) + EPS) * ln_g + ln_b
    vb = v.astype(BF16)
    nblk = uv.shape[0] // CHUNK
    gw = D_CHUNK // N_CHUNK_GROUPS
    rows = []
    for r in range(nblk):
        cols = []
        for g in range(N_CHUNK_GROUPS):
            vg = vb[r * CHUNK:(r + 1) * CHUNK, g * gw:(g + 1) * gw]
            cols.append(jnp.dot(wm[g], vg, preferred_element_type=F32))
        rows.append(jnp.concatenate(cols, axis=-1) + bs_full)
    s = jnp.concatenate(rows, axis=0) if nblk > 1 else rows[0]
    return u * s, v


def _lru_inputs(xc, lam, w_ai, b_a, b_i):
    ri = _mm(xc, w_ai)
    r = jax.nn.sigmoid(ri[:, :D_LRU] + b_a)
    i = jax.nn.sigmoid(ri[:, D_LRU:] + b_i)
    log_sig = jnp.minimum(lam, 0.0) - jnp.log1p(jnp.exp(-jnp.abs(lam)))
    log_a = LRU_C * r * log_sig
    a = jnp.exp(log_a)
    t = jnp.tanh(log_a)
    one_m_a2 = -2.0 * t / (1.0 - t)
    return a, jnp.sqrt(one_m_a2) * (i * xc)


def _conv(taps, conv_w, conv_b):
    out = conv_b + conv_w[3:4] * taps[0]
    for d in range(1, CONV_W):
        out = out + conv_w[3 - d:4 - d] * taps[d]
    return out


def _head_norm(x, bd, g):
    ms = jnp.dot((x * x).astype(BF16), bd, preferred_element_type=F32)
    return x * lax.rsqrt(ms + EPS) * g


def _merge(x, a_out, b_out, c_out, gl, wpa, wpb, wpc, wout):
    gates = jax.nn.sigmoid(gl)
    merged = (gates[:, :D_MODEL] * _mm(a_out, wpa)
              + gates[:, D_MODEL:2 * D_MODEL] * _mm(b_out, wpb)
              + gates[:, 2 * D_MODEL:] * _mm(c_out, wpc))
    return x + _mm(merged, wout)


def _mixer_prompt_kernel(sinks_ref, x_ref, n1_ref, win_ref, cws_ref, bs_ref, lng_ref, lnb_ref,
                         cw_ref, cb_ref, lam_ref, wai_ref, ba_ref, bi_ref, bdq_ref, bdk_ref,
                         gq_ref, gk_ref, bias_ref, wpa_ref, wpb_ref, wpc_ref, wout_ref,
                         y_ref, kk_ref, vk_ref, lx_ref, h_ref,
                         xpad_ref, hc_ref, pk_ref, pv_ref, *, bb):
    c = pl.program_id(1)
    m = bb * CHUNK

    @pl.when(c == 0)
    def _():
        xpad_ref[:, 0:8, :] = jnp.zeros((bb, 8, D_LRU), F32)
        hc_ref[...] = jnp.zeros_like(hc_ref)
        pk_ref[...] = jnp.zeros_like(pk_ref)
        pv_ref[...] = jnp.zeros_like(pv_ref)

    x = x_ref[...].reshape(m, D_MODEL)
    xn = _rms(x, n1_ref[...]).astype(BF16)

    def proj(lo, hi):
        return jnp.dot(xn, win_ref[:, lo:hi], preferred_element_type=F32)

    ri = lax.broadcasted_iota(jnp.int32, (CHUNK, CHUNK), 0)
    ci = lax.broadcasted_iota(jnp.int32, (CHUNK, CHUNK), 1)
    wm = [jnp.where(ri >= ci, cws_ref[g], 0.0).astype(BF16) for g in range(N_CHUNK_GROUPS)]
    a_out, _ = _chunk_branch(proj(O_UV, O_LX), wm, bs_ref[...], lng_ref[...], lnb_ref[...])

    lx = proj(O_LX, O_LG)
    xpad_ref[:, 8:8 + CHUNK, :] = lx.reshape(bb, CHUNK, D_LRU)
    taps = [lx] + [xpad_ref[:, 8 - d:8 - d + CHUNK, :].reshape(m, D_LRU) for d in range(1, CONV_W)]
    xc = _conv(taps, cw_ref[...], cb_ref[...])
    a, b = _lru_inputs(xc, lam_ref[...], wai_ref[...], ba_ref[...], bi_ref[...])
    a, b = _seg_scan(a, b, CHUNK)
    hs = []
    for r in range(bb):
        rows = slice(r * CHUNK, (r + 1) * CHUNK)
        h_r = a[rows] * hc_ref[r] + b[rows]
        hc_ref[r] = h_r[CHUNK - 1:CHUNK]
        hs.append(h_r)
    hs = jnp.concatenate(hs, axis=0) if bb > 1 else hs[0]
    b_out = _gelu(proj(O_LG, O_Q)) * hs
    tail = xpad_ref[:, CHUNK:CHUNK + 8, :]
    xpad_ref[:, 0:8, :] = tail
    lx_ref[...] = tail
    h_ref[...] = hc_ref[...]

    qn = _head_norm(proj(O_Q, O_KK), bdq_ref[...], gq_ref[...]) * (HEAD_DIM ** -0.5)
    kkn = _head_norm(proj(O_KK, O_VV), bdk_ref[...], gk_ref[...])
    vv = proj(O_VV, O_G)
    lane = lax.broadcasted_iota(jnp.int32, (1, LANES), 1)
    lane_lo = lane < HEAD_DIM
    kcol = lax.broadcasted_iota(jnp.int32, (1, 2 * WINDOW), 1)
    pen = jnp.where(kcol < WINDOW, jnp.where(c == 0, NEG, 0.0), 0.0)
    c_rows = []
    for r in range(bb):
        rows = slice(r * CHUNK, (r + 1) * CHUNK)
        kcat = jnp.concatenate([pk_ref[r], kkn[rows]], axis=0).astype(BF16)
        vcat = jnp.concatenate([pv_ref[r], vv[rows]], axis=0).astype(BF16)
        cols = []
        for j in range(N_HEADS // 2):
            kv = j // 2
            qj = qn[rows, j * LANES:(j + 1) * LANES]
            kh = kcat[:, kv * LANES:(kv + 1) * LANES]
            vh = vcat[:, kv * LANES:(kv + 1) * LANES]
            outs = []
            for half in range(2):
                hd = 2 * j + half
                qm = jnp.where(lane_lo if half == 0 else jnp.logical_not(lane_lo), qj, 0.0).astype(BF16)
                lg = lax.dot_general(qm, kh, (((1,), (1,)), ((), ())), preferred_element_type=F32)
                lg = lg + bias_ref[hd] + pen
                sink = sinks_ref[hd]
                mx = jnp.maximum(jnp.max(lg, axis=-1, keepdims=True), sink)
                p = jnp.exp(lg - mx)
                den = jnp.sum(p, axis=-1, keepdims=True) + jnp.exp(sink - mx)
                outs.append(jnp.dot(p.astype(BF16), vh, preferred_element_type=F32) / den)
            cols.append(jnp.where(lane_lo, outs[0], outs[1]))
        c_rows.append(jnp.concatenate(cols, axis=-1))
        pk_ref[r] = kkn[rows]
        pv_ref[r] = vv[rows]
    c_out = jnp.concatenate(c_rows, axis=0) if bb > 1 else c_rows[0]
    kk_ref[...] = jnp.where(lane_lo, kkn[:, :LANES], kkn[:, LANES:]).reshape(bb, CHUNK, LANES)
    vk_ref[...] = jnp.where(lane_lo, vv[:, :LANES], vv[:, LANES:]).reshape(bb, CHUNK, LANES)

    y = _merge(x, a_out, b_out, c_out, proj(O_G, D_INP),
               wpa_ref[...], wpb_ref[...], wpc_ref[...], wout_ref[...])
    y_ref[...] = y.reshape(bb, CHUNK, D_MODEL)


def _mixer_prompt(x, lw, bb=4):
    bsz, t, _ = x.shape
    nc = t // CHUNK
    assert t % CHUNK == 0 and bsz % bb == 0
    seq_blk = lambda w: pl.BlockSpec((bb, CHUNK, w), lambda i, c: (i, c, 0))
    keep_blk = lambda r, w: pl.BlockSpec((bb, r, w), lambda i, c: (i, 0, 0))
    consts = [lw['n1'], lw['win'], lw['cws_p'], lw['bs_p'], lw['lng'], lw['lnb'], lw['cw'], lw['cb'],
              lw['lam'], lw['wai'], lw['ba'], lw['bi'], lw['bdq'], lw['bdk'], lw['gq'], lw['gk'],
              lw['bias_p'], lw['wpa'], lw['wpb'], lw['wpc'], lw['wout']]
    out_shape = [jax.ShapeDtypeStruct((bsz, t, D_MODEL), F32),
                 jax.ShapeDtypeStruct((bsz, WINDOW, LANES), F32),
                 jax.ShapeDtypeStruct((bsz, WINDOW, LANES), F32),
                 jax.ShapeDtypeStruct((bsz, 8, D_LRU), F32),
                 jax.ShapeDtypeStruct((bsz, 1, D_LRU), F32)]
    return pl.pallas_call(
        functools.partial(_mixer_prompt_kernel, bb=bb),
        grid=(bsz // bb, nc),
        in_specs=[pl.BlockSpec(memory_space=pltpu.SMEM), seq_blk(D_MODEL)] + [_resident(a.shape) for a in consts],
        out_specs=[seq_blk(D_MODEL), keep_blk(WINDOW, LANES), keep_blk(WINDOW, LANES),
                   keep_blk(8, D_LRU), keep_blk(1, D_LRU)],
        out_shape=out_shape,
        scratch_shapes=[pltpu.VMEM((bb, CHUNK + 8, D_LRU), F32),
                        pltpu.VMEM((bb, 1, D_LRU), F32),
                        pltpu.VMEM((bb, CHUNK, 2 * LANES), F32),
                        pltpu.VMEM((bb, CHUNK, 2 * LANES), F32)],
        compiler_params=pltpu.CompilerParams(dimension_semantics=("parallel", "arbitrary"),
                                             vmem_limit_bytes=VMEM_LIMIT),
        name="mixer_prompt",
    )(lw['sinks'], x, *consts)


def _mixer_sample_kernel(x_ref, ck_ref, cv_ref, cs_ref, h0_ref, n1_ref, win_ref, cws_ref, bs_ref,
                         lng_ref, lnb_ref, cw_ref, cb_ref, lam_ref, wai_ref, ba_ref, bi_ref,
                         bdq_ref, bdk_ref, gq_ref, gk_ref, bias_ref, sink_ref,
                         wpa_ref, wpb_ref, wpc_ref, wout_ref,
                         y_ref, kk_ref, vk_ref, lx_ref, h_ref, cvn_ref,
                         xpad_ref, *, sb, t):
    m = sb * t

    x = x_ref[...].reshape(m, D_MODEL)
    xn = _rms(x, n1_ref[...]).astype(BF16)

    def proj(lo, hi):
        return jnp.dot(xn, win_ref[:, lo:hi], preferred_element_type=F32)

    ri = lax.broadcasted_iota(jnp.int32, (CHUNK, CHUNK), 0)
    ci = lax.broadcasted_iota(jnp.int32, (CHUNK, CHUNK), 1)
    keep = ((ri // t) == (ci // t)) & ((ci % t) <= (ri % t))
    wm = [jnp.where(keep, cws_ref[g], 0.0).astype(BF16) for g in range(N_CHUNK_GROUPS)]
    a_out, v_norm = _chunk_branch(proj(O_UV, O_LX), wm, bs_ref[...], lng_ref[...], lnb_ref[...])
    cvn_ref[...] = v_norm.reshape(sb, t, D_CHUNK)

    lx = proj(O_LX, O_LG)
    xpad_ref[:, 0:8, :] = cs_ref[...]
    xpad_ref[:, 8:8 + t, :] = lx.reshape(sb, t, D_LRU)
    taps = [lx] + [xpad_ref[:, 8 - d:8 - d + t, :].reshape(m, D_LRU) for d in range(1, CONV_W)]
    xc = _conv(taps, cw_ref[...], cb_ref[...])
    a, b = _lru_inputs(xc, lam_ref[...], wai_ref[...], ba_ref[...], bi_ref[...])
    a, b = _seg_scan(a, b, t)
    h0 = jnp.broadcast_to(h0_ref[...], (sb, t, D_LRU)).reshape(m, D_LRU)
    hs = a * h0 + b
    b_out = _gelu(proj(O_LG, O_Q)) * hs
    lx_ref[...] = lx.reshape(sb, t, D_LRU)
    h_ref[...] = hs.reshape(sb, t, D_LRU)

    qn = _head_norm(proj(O_Q, O_KK), bdq_ref[...], gq_ref[...]) * (HEAD_DIM ** -0.5)
    kkn = _head_norm(proj(O_KK, O_VV), bdk_ref[...], gk_ref[...])
    vv = proj(O_VV, O_G)
    lane = lax.broadcasted_iota(jnp.int32, (1, 1, LANES), 2)
    lane_lo = lane < HEAD_DIM
    lane_lo2 = lane_lo.reshape(1, LANES)
    knew = jnp.where(lane_lo2, kkn[:, :LANES], kkn[:, LANES:]).reshape(sb, t, LANES)
    vnew = jnp.where(lane_lo2, vv[:, :LANES], vv[:, LANES:]).reshape(sb, t, LANES)
    kc = jnp.concatenate([ck_ref[...], knew], axis=1)
    vc = jnp.concatenate([cv_ref[...], vnew], axis=1)
    kk_ref[...] = kc[:, t:, :]
    vk_ref[...] = vc[:, t:, :]
    q3 = qn.reshape(sb, t, D_ATTN)
    q3r = pltpu.roll(qn, HEAD_DIM, 1).reshape(sb, t, D_ATTN)
    pieces = []
    for hd in range(N_HEADS):
        kv = hd // (N_HEADS // N_KV_HEADS)
        if hd % 2 == kv:
            src, col = q3, hd // 2
        else:
            src, col = q3r, (hd + 1) // 2
        msk = lane_lo if kv == 0 else jnp.logical_not(lane_lo)
        pieces.append(jnp.where(msk, src[:, :, col * LANES:(col + 1) * LANES], 0.0))
    qs = jnp.concatenate(pieces, axis=1).astype(BF16)
    lg = jnp.einsum('bqd,bkd->bqk', qs, kc.astype(BF16), preferred_element_type=F32)
    lg = lg + bias_ref[...]
    sink = sink_ref[:, 0:1]
    mx = jnp.maximum(jnp.max(lg, axis=-1, keepdims=True), sink)
    p = jnp.exp(lg - mx)
    den = jnp.sum(p, axis=-1, keepdims=True) + jnp.exp(sink - mx)
    o = jnp.einsum('bqk,bkd->bqd', p.astype(BF16), vc.astype(BF16), preferred_element_type=F32) / den
    o_r = pltpu.roll(o.reshape(sb * N_HEADS * t, LANES), HEAD_DIM, 1).reshape(sb, N_HEADS * t, LANES)
    cols = []
    for j in range(N_HEADS // 2):
        kv = j // 2
        lo_src = o if kv == 0 else o_r
        hi_src = o_r if kv == 0 else o
        cols.append(jnp.where(lane_lo, lo_src[:, 2 * j * t:(2 * j + 1) * t, :],
                              hi_src[:, (2 * j + 1) * t:(2 * j + 2) * t, :]))
    c_out = jnp.concatenate(cols, axis=-1).reshape(m, D_ATTN)

    y = _merge(x, a_out, b_out, c_out, proj(O_G, D_INP),
               wpa_ref[...], wpb_ref[...], wpc_ref[...], wout_ref[...])
    y_ref[...] = y.reshape(sb, t, D_MODEL)


def _mixer_sample(x, ck, cv, cs, h0, lw, sb=32):
    nb, t, _ = x.shape
    assert nb % sb == 0 and t == 8 and (sb * t) % CHUNK == 0
    blk = lambda r, w: pl.BlockSpec((sb, r, w), lambda i: (i, 0, 0))
    consts = [lw['n1'], lw['win'], lw['cws_s'], lw['bs_s'], lw['lng'], lw['lnb'], lw['cw'], lw['cb'],
              lw['lam'], lw['wai'], lw['ba'], lw['bi'], lw['bdq'], lw['bdk'], lw['gq'], lw['gk'],
              lw['bias_s'], lw['sink_s'], lw['wpa'], lw['wpb'], lw['wpc'], lw['wout']]
    out_shape = [jax.ShapeDtypeStruct((nb, t, D_MODEL), F32),
                 jax.ShapeDtypeStruct((nb, WINDOW, LANES), F32),
                 jax.ShapeDtypeStruct((nb, WINDOW, LANES), F32),
                 jax.ShapeDtypeStruct((nb, t, D_LRU), F32),
                 jax.ShapeDtypeStruct((nb, t, D_LRU), F32),
                 jax.ShapeDtypeStruct((nb, t, D_CHUNK), F32)]
    return pl.pallas_call(
        functools.partial(_mixer_sample_kernel, sb=sb, t=t),
        grid=(nb // sb,),
        in_specs=[blk(t, D_MODEL), blk(WINDOW, LANES), blk(WINDOW, LANES), blk(8, D_LRU), blk(1, D_LRU)]
                 + [_resident(a.shape) for a in consts],
        out_specs=[blk(t, D_MODEL), blk(WINDOW, LANES), blk(WINDOW, LANES),
                   blk(t, D_LRU), blk(t, D_LRU), blk(t, D_CHUNK)],
        out_shape=out_shape,
        scratch_shapes=[pltpu.VMEM((sb, 8 + t, D_LRU), F32)],
        compiler_params=pltpu.CompilerParams(dimension_semantics=("parallel",),
                                             vmem_limit_bytes=VMEM_LIMIT),
        name="mixer_sample",
    )(x, ck, cv, cs, h0, *consts)


def _ffn_kernel(x_ref, g_ref, wg_ref, wu_ref, wd_ref, o_ref, xn_ref):
    f = pl.program_id(1)

    @pl.when(f == 0)
    def _():
        x = x_ref[...]
        xn_ref[...] = _rms(x, g_ref[...]).astype(BF16)
        o_ref[...] = x

    xn = xn_ref[...]
    gate = jnp.dot(xn, wg_ref[...], preferred_element_type=F32)
    up = jnp.dot(xn, wu_ref[...], preferred_element_type=F32)
    o_ref[...] += _mm(gate * jax.nn.sigmoid(gate) * up, wd_ref[...])


def _ffn(x, g, wg, wu, wd, tm=1024, tf=896):
    n = x.shape[0]
    tm = min(tm, n)
    assert n % tm == 0 and D_FF % tf == 0
    return pl.pallas_call(
        _ffn_kernel,
        grid=(n // tm, D_FF // tf),
        in_specs=[pl.BlockSpec((tm, D_MODEL), lambda i, f: (i, 0)),
                  _resident(g.shape),
                  pl.BlockSpec((D_MODEL, tf), lambda i, f: (0, f)),
                  pl.BlockSpec((D_MODEL, tf), lambda i, f: (0, f)),
                  pl.BlockSpec((tf, D_MODEL), lambda i, f: (f, 0))],
        out_specs=pl.BlockSpec((tm, D_MODEL), lambda i, f: (i, 0)),
        out_shape=jax.ShapeDtypeStruct((n, D_MODEL), F32),
        scratch_shapes=[pltpu.VMEM((tm, D_MODEL), BF16)],
        compiler_params=pltpu.CompilerParams(dimension_semantics=("parallel", "arbitrary"),
                                             vmem_limit_bytes=VMEM_LIMIT),
        name="ffn_dense",
    )(x, g, wg, wu, wd)


def _router_kernel(x_ref, g_ref, wr_ref, tri_ref, xn_ref, comb_ref, rank_ref, combt_ref, rankt_ref, cnt_ref):
    xn = _rms(x_ref[...], g_ref[...])
    xn_ref[...] = xn.astype(BF16)
    logits = jnp.dot(xn, wr_ref[...], precision=lax.Precision.HIGHEST, preferred_element_type=F32)
    lane = lax.broadcasted_iota(jnp.int32, logits.shape, 1).astype(F32)
    logits = jnp.where(lane < N_EXPERTS, logits, NEG)
    m1 = jnp.max(logits, axis=-1, keepdims=True)
    i1 = jnp.min(jnp.where(logits == m1, lane, float(LANES)), axis=-1, keepdims=True)
    rest = jnp.where(lane == i1, NEG, logits)
    m2 = jnp.max(rest, axis=-1, keepdims=True)
    i2 = jnp.min(jnp.where(rest == m2, lane, float(LANES)), axis=-1, keepdims=True)
    e2 = jnp.exp(m2 - m1)
    w1 = 1.0 / (1.0 + e2)
    comb = jnp.where(lane == i1, w1, 0.0) + jnp.where(lane == i2, e2 * w1, 0.0)
    sel = jnp.where(comb > 0.0, 1.0, 0.0)
    nsub = x_ref.shape[0] // MOE_SUB_TOKENS
    selb = sel.astype(BF16)
    ranks, cnts = [], []
    for h in range(nsub):
        rows = slice(h * MOE_SUB_TOKENS, (h + 1) * MOE_SUB_TOKENS)
        ranks.append(jnp.dot(tri_ref[...], selb[rows], preferred_element_type=F32))
        cnts.append(jnp.sum(sel[rows], axis=0, keepdims=True))
    rank = jnp.concatenate(ranks, axis=0) if nsub > 1 else ranks[0]
    cnts = cnts + [jnp.zeros((8 - nsub, LANES), F32)]
    comb_ref[...] = comb
    rank_ref[...] = rank
    combt_ref[...] = comb.T[:N_EXPERTS]
    rankt_ref[...] = rank.T[:N_EXPERTS]
    cnt_ref[...] = jnp.concatenate(cnts, axis=0)[None]


def _router(x, g, wr, tri, tb):
    n = x.shape[0]
    nblk = n // tb
    row = lambda w: pl.BlockSpec((tb, w), lambda i: (i, 0))
    colb = pl.BlockSpec((N_EXPERTS, tb), lambda i: (0, i))
    return pl.pallas_call(
        _router_kernel,
        grid=(nblk,),
        in_specs=[row(D_MODEL), _resident(g.shape), _resident(wr.shape), _resident(tri.shape)],
        out_specs=[row(D_MODEL), row(LANES), row(LANES), colb, colb,
                   pl.BlockSpec((1, 8, LANES), lambda i: (i, 0, 0))],
        out_shape=[jax.ShapeDtypeStruct((n, D_MODEL), BF16),
                   jax.ShapeDtypeStruct((n, LANES), F32),
                   jax.ShapeDtypeStruct((n, LANES), F32),
                   jax.ShapeDtypeStruct((N_EXPERTS, n), F32),
                   jax.ShapeDtypeStruct((N_EXPERTS, n), F32),
                   jax.ShapeDtypeStruct((nblk, 8, LANES), F32)],
        compiler_params=pltpu.CompilerParams(dimension_semantics=("parallel",),
                                             vmem_limit_bytes=VMEM_LIMIT),
        name="moe_router",
    )(x, g, wr, tri)


def _moe_kernel(cnt_ref, xn_ref, comb_ref, rank_ref, combt_ref, rankt_ref, wg_ref, wu_ref, wd_ref,
                yin_ref, yout_ref, *, nblk, tb):
    e = pl.program_id(0)
    blk = pl.program_id(1)
    nsub = tb // MOE_SUB_TOKENS
    cap = MOE_SUB_ROWS
    base = (e * nblk + blk) * nsub
    ntile = jnp.int32(0)
    for h in range(nsub):
        ntile = jnp.maximum(ntile, lax.div(cnt_ref[base + h] + (cap - 1), jnp.int32(cap)))

    lane_e = lax.broadcasted_iota(jnp.int32, (tb, LANES), 1) == e
    cw_col = jnp.sum(jnp.where(lane_e, comb_ref[...], 0.0), axis=-1, keepdims=True)
    rk_col = jnp.sum(jnp.where(lane_e, rank_ref[...], 0.0), axis=-1, keepdims=True)
    sub_e = lax.broadcasted_iota(jnp.int32, (N_EXPERTS, tb), 0) == e
    cw_row = jnp.sum(jnp.where(sub_e, combt_ref[...], 0.0), axis=0, keepdims=True)
    rk_row = jnp.sum(jnp.where(sub_e, rankt_ref[...], 0.0), axis=0, keepdims=True)
    rk_col = jnp.where(cw_col > 0.0, rk_col, -1.0)
    rk_row = jnp.where(cw_row > 0.0, rk_row, -1.0)
    sub_i = lax.broadcasted_iota(jnp.int32, (cap, MOE_SUB_TOKENS), 0).astype(F32)
    lane_i = lax.broadcasted_iota(jnp.int32, (MOE_SUB_TOKENS, cap), 1).astype(F32)

    def tile(j, first):
        j0 = (j * cap).astype(F32)
        xg = []
        for h in range(nsub):
            tok = slice(h * MOE_SUB_TOKENS, (h + 1) * MOE_SUB_TOKENS)
            gather = jnp.where(rk_row[:, tok] - j0 == sub_i, 1.0, 0.0).astype(BF16)
            xg.append(jnp.dot(gather, xn_ref[tok, :], preferred_element_type=F32))
        xg = (jnp.concatenate(xg, axis=0) if nsub > 1 else xg[0]).astype(BF16)
        gate = jnp.dot(xg, wg_ref[0], preferred_element_type=F32)
        up = jnp.dot(xg, wu_ref[0], preferred_element_type=F32)
        o = _mm(gate * jax.nn.sigmoid(gate) * up, wd_ref[0]).astype(BF16)
        for h in range(nsub):
            tok = slice(h * MOE_SUB_TOKENS, (h + 1) * MOE_SUB_TOKENS)
            scatter = jnp.where(rk_col[tok] - j0 == lane_i, 1.0, 0.0).astype(BF16)
            part = cw_col[tok] * jnp.dot(scatter, o[h * cap:(h + 1) * cap], preferred_element_type=F32)
            src_ref = yin_ref if first else yout_ref
            yout_ref[tok, :] = src_ref[tok, :] + part

    def loop_body(j, carry):
        tile(j, False)
        return carry

    if nblk == 1:
        @pl.when(e == 0)
        def _():
            yout_ref[...] = yin_ref[...]
        lax.fori_loop(0, ntile, loop_body, 0)
    else:
        tile(jnp.int32(0), True)
        lax.fori_loop(1, ntile, loop_body, 0)


def _moe(cnt, xn, comb, rank, combt, rankt, wg, wu, wd, y, tb):
    n = xn.shape[0]
    nblk = n // tb
    row = lambda w: pl.BlockSpec((tb, w), lambda e, b, c: (b, 0))
    colb = pl.BlockSpec((N_EXPERTS, tb), lambda e, b, c: (0, b))
    wspec = lambda r, w: pl.BlockSpec((1, r, w), lambda e, b, c: (e, 0, 0), pipeline_mode=pl.Buffered(1))
    grid_spec = pltpu.PrefetchScalarGridSpec(
        num_scalar_prefetch=1,
        grid=(N_EXPERTS, nblk),
        in_specs=[row(D_MODEL), row(LANES), row(LANES), colb, colb,
                  wspec(D_MODEL, D_FF), wspec(D_MODEL, D_FF), wspec(D_FF, D_MODEL), row(D_MODEL)],
        out_specs=row(D_MODEL),
    )
    return pl.pallas_call(
        functools.partial(_moe_kernel, nblk=nblk, tb=tb),
        grid_spec=grid_spec,
        out_shape=jax.ShapeDtypeStruct((n, D_MODEL), F32),
        input_output_aliases={9: 0},
        compiler_params=pltpu.CompilerParams(dimension_semantics=("arbitrary", "arbitrary"),
                                             vmem_limit_bytes=VMEM_LIMIT),
        name="moe_experts",
    )(cnt, xn, comb, rank, combt, rankt, wg, wu, wd, y)


def _moe_ffn(x, g, wr, tri, wg, wu, wd, tb):
    xn, comb, rank, combt, rankt, cnt = _router(x, g, wr, tri, tb)
    nsub = tb // MOE_SUB_TOKENS
    cnt = jnp.transpose(cnt[:, :nsub, :N_EXPERTS].astype(jnp.int32), (2, 0, 1)).reshape(-1)
    return _moe(cnt, xn, comb, rank, combt, rankt, wg, wu, wd, x, tb)


def _block_diag_mean(n, width):
    i = jnp.arange(n) // width
    return jnp.where(i[:, None] == i[None, :], 1.0 / width, 0.0).astype(BF16)


def _alibi_bias(nq, nk):
    dist = (WINDOW + jnp.arange(nq)[:, None] - jnp.arange(nk)[None, :]).astype(F32)
    allowed = (dist >= 0) & (dist <= WINDOW)
    slopes = jnp.exp2(-8.0 * (jnp.arange(N_HEADS, dtype=F32) + 1.0) / N_HEADS)
    return jnp.where(allowed[None], -slopes[:, None, None] * dist[None], NEG)


def _pack_layer(l, p, t_s):
    w_in = p['w_in'][l]
    dup = lambda w: jnp.concatenate([w[:, i * HEAD_DIM:(i + 1) * HEAD_DIM] for i in range(N_KV_HEADS) for _ in (0, 1)], axis=1)
    s_q, s_k = 2 * D_CHUNK + 2 * D_LRU, 2 * D_CHUNK + 2 * D_LRU + D_ATTN
    s_v = s_k + N_KV_HEADS * HEAD_DIM
    s_g = s_v + N_KV_HEADS * HEAD_DIM
    win = jnp.concatenate([w_in[:, :s_k], dup(w_in[:, s_k:s_v]), dup(w_in[:, s_v:s_g]), w_in[:, s_g:]], axis=1)
    assert win.shape[1] == D_INP
    nblk = D_LRU // p['lru_wa'].shape[-1]
    bd = lambda w: jax.scipy.linalg.block_diag(*[w[i] for i in range(nblk)])
    gw = D_CHUNK // N_CHUNK_GROUPS
    rep = CHUNK // t_s
    bias_s = _alibi_bias(t_s, WINDOW + t_s).reshape(N_HEADS * t_s, WINDOW + t_s)
    return {
        'n1': p['norm1_g'][l][None], 'win': win.astype(BF16),
        'cws_p': p['chunk_ws'][l],
        'cws_s': jnp.tile(p['chunk_ws'][l][:, :t_s, :t_s], (1, rep, rep)),
        'bs_p': jnp.repeat(p['chunk_bs'][l].T, gw, axis=1),
        'bs_s': jnp.tile(jnp.repeat(p['chunk_bs'][l][:, :t_s].T, gw, axis=1), (rep, 1)),
        'lng': p['chunk_ln_g'][l][None], 'lnb': p['chunk_ln_b'][l][None],
        'cw': p['conv_w'][l], 'cb': p['conv_b'][l][None], 'lam': p['lru_lambda'][l][None],
        'wai': jnp.concatenate([bd(p['lru_wa'][l]), bd(p['lru_wi'][l])], axis=1).astype(BF16),
        'ba': p['lru_ba'][l][None], 'bi': p['lru_bi'][l][None],
        'bdq': _block_diag_mean(D_ATTN, HEAD_DIM), 'bdk': _block_diag_mean(2 * LANES, LANES),
        'gq': jnp.tile(p['q_norm_g'][l], N_HEADS)[None],
        'gk': jnp.tile(p['k_norm_g'][l], 2 * N_KV_HEADS)[None],
        'bias_p': _alibi_bias(CHUNK, 2 * WINDOW), 'bias_s': bias_s,
        'sinks': p['attn_sinks'][l],
        'sink_s': jnp.broadcast_to(jnp.repeat(p['attn_sinks'][l], t_s)[:, None], (N_HEADS * t_s, LANES)),
        'wpa': p['w_proj_a'][l].astype(BF16), 'wpb': p['w_proj_b'][l].astype(BF16),
        'wpc': p['w_proj_c'][l].astype(BF16), 'wout': p['w_out'][l].astype(BF16),
    }


def _strict_lower(n):
    i = jnp.arange(n)
    return jnp.where(i[None, :] < i[:, None], 1.0, 0.0).astype(BF16)


def kernel(x_prompt, x_sample, cache_win_k, cache_win_v, state_conv, state_lru_h, norm1_g, w_in, chunk_ln_g, chunk_ln_b, chunk_ws, chunk_bs, conv_w, conv_b, lru_lambda, lru_wa, lru_ba, lru_wi, lru_bi, q_norm_g, k_norm_g, attn_sinks, w_proj_a, w_proj_b, w_proj_c, w_out, norm2_g, ffn_w_gate, ffn_w_up, ffn_w_down, moe_router, moe_w_gate, moe_w_up, moe_w_down):
    p = dict(norm1_g=norm1_g, w_in=w_in, chunk_ln_g=chunk_ln_g, chunk_ln_b=chunk_ln_b, chunk_ws=chunk_ws,
             chunk_bs=chunk_bs, conv_w=conv_w, conv_b=conv_b, lru_lambda=lru_lambda, lru_wa=lru_wa,
             lru_ba=lru_ba, lru_wi=lru_wi, lru_bi=lru_bi, q_norm_g=q_norm_g, k_norm_g=k_norm_g,
             attn_sinks=attn_sinks, w_proj_a=w_proj_a, w_proj_b=w_proj_b, w_proj_c=w_proj_c, w_out=w_out)
    depth = w_in.shape[0]
    bp, tp, _ = x_prompt.shape
    bs, ts, _ = x_sample.shape
    tb_p = min(1024, bp * tp)
    tb_s = min(1024, bs * ts)
    assert tb_p % MOE_SUB_TOKENS == 0 and tb_s % MOE_SUB_TOKENS == 0
    tri = _strict_lower(MOE_SUB_TOKENS)

    yp, ys = x_prompt, x_sample
    outs = {k: [] for k in ('pk', 'pv', 'pconv', 'ph', 'sk', 'sv', 'sconv', 'sh', 'schunk')}
    for l in range(depth):
        lw = _pack_layer(l, p, ts)
        yp, pk, pv, plx, phl = _mixer_prompt(yp, lw)
        ck = cache_win_k[l].reshape(bs, WINDOW, LANES)
        cv = cache_win_v[l].reshape(bs, WINDOW, LANES)
        cs = jnp.pad(state_conv[l], ((0, 0), (8 - (CONV_W - 1), 0), (0, 0)))
        ys, sk, sv, slx, shl, scv = _mixer_sample(ys, ck, cv, cs, state_lru_h[l][:, None, :], lw)
        outs['pk'].append(pk.reshape(bp, WINDOW, N_KV_HEADS, HEAD_DIM))
        outs['pv'].append(pv.reshape(bp, WINDOW, N_KV_HEADS, HEAD_DIM))
        outs['pconv'].append(plx[:, 8 - (CONV_W - 1):, :])
        outs['ph'].append(phl[:, 0, :])
        outs['sk'].append(sk.reshape(bs, WINDOW, N_KV_HEADS, HEAD_DIM))
        outs['sv'].append(sv.reshape(bs, WINDOW, N_KV_HEADS, HEAD_DIM))
        outs['sconv'].append(slx[:, ts - (CONV_W - 1):, :])
        outs['sh'].append(shl[:, ts - 1, :])
        outs['schunk'].append(scv)

        g2 = norm2_g[l][None]
        flat_p, flat_s = yp.reshape(bp * tp, D_MODEL), ys.reshape(bs * ts, D_MODEL)
        j = l // 2
        if l % 2 == 0:
            wg, wu, wd = ffn_w_gate[j].astype(BF16), ffn_w_up[j].astype(BF16), ffn_w_down[j].astype(BF16)
            flat_p = _ffn(flat_p, g2, wg, wu, wd)
            flat_s = _ffn(flat_s, g2, wg, wu, wd)
        else:
            wg, wu, wd = moe_w_gate[j].astype(BF16), moe_w_up[j].astype(BF16), moe_w_down[j].astype(BF16)
            wr = jnp.pad(moe_router[j], ((0, 0), (0, LANES - N_EXPERTS)))
            flat_p = _moe_ffn(flat_p, g2, wr, tri, wg, wu, wd, tb_p)
            flat_s = _moe_ffn(flat_s, g2, wr, tri, wg, wu, wd, tb_s)
        yp, ys = flat_p.reshape(bp, tp, D_MODEL), flat_s.reshape(bs, ts, D_MODEL)

    st = {k: jnp.stack(v) for k, v in outs.items()}
    return (yp, ys, st['pk'], st['pv'], st['pconv'], st['ph'],
            st['sk'], st['sv'], st['sconv'], st['sh'], st['schunk'])
```

```python
import functools
import math

import jax
import jax.numpy as jnp
from jax import lax
from jax.experimental import pallas as pl
from jax.experimental.pallas import tpu as pltpu

F32 = jnp.float32
BF16 = jnp.bfloat16

D_MODEL = 1024
D_CHUNK = 512
CHUNK = 128
N_CHUNK_GROUPS = 4
D_LRU = 512
CONV_W = 4
LRU_C = 8.0
N_HEADS = 8
N_KV_HEADS = 2
HEAD_DIM = 64
D_ATTN = N_HEADS * HEAD_DIM
WINDOW = 128
D_FF = 3584
N_EXPERTS = 8
EPS = 1e-6
NEG = -1e30

O_UV = 0
O_LX = 2 * D_CHUNK
O_LG = O_LX + D_LRU
O_Q = O_LG + D_LRU
O_KK = O_Q + D_ATTN
O_VV = O_KK + 2 * N_KV_HEADS * HEAD_DIM
O_G = O_VV + 2 * N_KV_HEADS * HEAD_DIM
D_INP = O_G + 3 * D_MODEL

LANES = 128
MOE_SUB_TOKENS = 512
MOE_PASS_ROWS = (96, 144, 192)
VMEM_LIMIT = 56 * 1024 * 1024


def _resident(shape):
    nd = len(shape)
    return pl.BlockSpec(shape, lambda *_: (0,) * nd, pipeline_mode=pl.Buffered(1))


def _mm(a, w):
    return jnp.dot(a.astype(BF16), w, preferred_element_type=F32)


def _gelu(x):
    return 0.5 * x * (1.0 + jnp.tanh(0.7978845608028654 * (x + 0.044715 * (x * x * x))))


def _rms(x, g):
    return x * lax.rsqrt(jnp.mean(x * x, axis=-1, keepdims=True) + EPS) * g


def _seg_scan(a, b, seg):
    pos = lax.broadcasted_iota(jnp.int32, (a.shape[0], 1), 0) & (seg - 1)
    s = 1
    while s < seg:
        ok = pos >= s
        a_sh = pltpu.roll(a, s, 0)
        b_sh = pltpu.roll(b, s, 0)
        b = jnp.where(ok, a * b_sh + b, b)
        a = jnp.where(ok, a * a_sh, a)
        s *= 2
    return a, b


def _chunk_branch(uv, wm, bs_full, ln_g, ln_b):
    uv = _gelu(uv)
    u = uv[:, :D_CHUNK]
    v = uv[:, D_CHUNK:]
    mu = jnp.mean(v, axis=-1, keepdims=True)
    vc = v - mu
    v = vc * lax.rsqrt(jnp.mean(vc * vc, axis=-1, keepdims=True) + EPS) * ln_g + ln_b
    vb = v.astype(BF16)
    nblk = uv.shape[0] // CHUNK
    gw = D_CHUNK // N_CHUNK_GROUPS
    rows = []
    for r in range(nblk):
        cols = []
        for g in range(N_CHUNK_GROUPS):
            vg = vb[r * CHUNK:(r + 1) * CHUNK, g * gw:(g + 1) * gw]
            cols.append(jnp.dot(wm[g], vg, preferred_element_type=F32))
        rows.append(jnp.concatenate(cols, axis=-1) + bs_full)
    s = jnp.concatenate(rows, axis=0) if nblk > 1 else rows[0]
    return u * s, v


def _lru_inputs(xc, lam, w_ai, b_a, b_i):
    ri = _mm(xc, w_ai)
    r = jax.nn.sigmoid(ri[:, :D_LRU] + b_a)
    i = jax.nn.sigmoid(ri[:, D_LRU:] + b_i)
    log_sig = jnp.minimum(lam, 0.0) - jnp.log1p(jnp.exp(-jnp.abs(lam)))
    log_a = LRU_C * r * log_sig
    a = jnp.exp(log_a)
    t = jnp.tanh(log_a)
    one_m_a2 = -2.0 * t / (1.0 - t)
    return a, jnp.sqrt(one_m_a2) * (i * xc)


def _conv(taps, conv_w, conv_b):
    out = conv_b + conv_w[3:4] * taps[0]
    for d in range(1, CONV_W):
        out = out + conv_w[3 - d:4 - d] * taps[d]
    return out


def _head_norm(x, bd, g):
    ms = jnp.dot((x * x).astype(BF16), bd, preferred_element_type=F32)
    return x * lax.rsqrt(ms + EPS) * g


def _merge(x, a_out, b_out, c_out, gl, wpa, wpb, wpc, wout):
    gates = jax.nn.sigmoid(gl)
    merged = (gates[:, :D_MODEL] * _mm(a_out, wpa)
              + gates[:, D_MODEL:2 * D_MODEL] * _mm(b_out, wpb)
              + gates[:, 2 * D_MODEL:] * _mm(c_out, wpc))
    return x + _mm(merged, wout)


def _mixer_prompt_kernel(sinks_ref, x_ref, n1_ref, win_ref, cws_ref, bs_ref, lng_ref, lnb_ref,
                         cw_ref, cb_ref, lam_ref, wai_ref, ba_ref, bi_ref, bdq_ref, bdk_ref,
                         gq_ref, gk_ref, bias_ref, wpa_ref, wpb_ref, wpc_ref, wout_ref,
                         y_ref, kk_ref, vk_ref, lx_ref, h_ref,
                         xpad_ref, hc_ref, pk_ref, pv_ref, *, bb):
    c = pl.program_id(1)
    m = bb * CHUNK

    @pl.when(c == 0)
    def _():
        xpad_ref[:, 0:8, :] = jnp.zeros((bb, 8, D_LRU), F32)
        hc_ref[...] = jnp.zeros_like(hc_ref)
        pk_ref[...] = jnp.zeros_like(pk_ref)
        pv_ref[...] = jnp.zeros_like(pv_ref)

    x = x_ref[...].reshape(m, D_MODEL)
    xn = _rms(x, n1_ref[...]).astype(BF16)

    def proj(lo, hi):
        return jnp.dot(xn, win_ref[:, lo:hi], preferred_element_type=F32)

    ri = lax.broadcasted_iota(jnp.int32, (CHUNK, CHUNK), 0)
    ci = lax.broadcasted_iota(jnp.int32, (CHUNK, CHUNK), 1)
    wm = [jnp.where(ri >= ci, cws_ref[g], 0.0).astype(BF16) for g in range(N_CHUNK_GROUPS)]
    a_out, _ = _chunk_branch(proj(O_UV, O_LX), wm, bs_ref[...], lng_ref[...], lnb_ref[...])

    lx = proj(O_LX, O_LG)
    xpad_ref[:, 8:8 + CHUNK, :] = lx.reshape(bb, CHUNK, D_LRU)
    taps = [lx] + [xpad_ref[:, 8 - d:8 - d + CHUNK, :].reshape(m, D_LRU) for d in range(1, CONV_W)]
    xc = _conv(taps, cw_ref[...], cb_ref[...])
    a, b = _lru_inputs(xc, lam_ref[...], wai_ref[...], ba_ref[...], bi_ref[...])
    a, b = _seg_scan(a, b, CHUNK)
    hs = []
    for r in range(bb):
        rows = slice(r * CHUNK, (r + 1) * CHUNK)
        h_r = a[rows] * hc_ref[r] + b[rows]
        hc_ref[r] = h_r[CHUNK - 1:CHUNK]
        hs.append(h_r)
    hs = jnp.concatenate(hs, axis=0) if bb > 1 else hs[0]
    b_out = _gelu(proj(O_LG, O_Q)) * hs
    tail = xpad_ref[:, CHUNK:CHUNK + 8, :]
    xpad_ref[:, 0:8, :] = tail
    lx_ref[...] = tail
    h_ref[...] = hc_ref[...]

    qn = _head_norm(proj(O_Q, O_KK), bdq_ref[...], gq_ref[...]) * (HEAD_DIM ** -0.5)
    kkn = _head_norm(proj(O_KK, O_VV), bdk_ref[...], gk_ref[...])
    vv = proj(O_VV, O_G)
    lane = lax.broadcasted_iota(jnp.int32, (1, LANES), 1)
    lane_lo = lane < HEAD_DIM
    kcol = lax.broadcasted_iota(jnp.int32, (1, 2 * WINDOW), 1)
    pen = jnp.where(kcol < WINDOW, jnp.where(c == 0, NEG, 0.0), 0.0)
    c_rows = []
    for r in range(bb):
        rows = slice(r * CHUNK, (r + 1) * CHUNK)
        kcat = jnp.concatenate([pk_ref[r], kkn[rows]], axis=0).astype(BF16)
        vcat = jnp.concatenate([pv_ref[r], vv[rows]], axis=0).astype(BF16)
        cols = []
        for j in range(N_HEADS // 2):
            kv = j // 2
            qj = qn[rows, j * LANES:(j + 1) * LANES]
            kh = kcat[:, kv * LANES:(kv + 1) * LANES]
            vh = vcat[:, kv * LANES:(kv + 1) * LANES]
            outs = []
            for half in range(2):
                hd = 2 * j + half
                qm = jnp.where(lane_lo if half == 0 else jnp.logical_not(lane_lo), qj, 0.0).astype(BF16)
                lg = lax.dot_general(qm, kh, (((1,), (1,)), ((), ())), preferred_element_type=F32)
                lg = lg + bias_ref[hd] + pen
                sink = sinks_ref[hd]
                mx = jnp.maximum(jnp.max(lg, axis=-1, keepdims=True), sink)
                p = jnp.exp(lg - mx)
                den = jnp.sum(p, axis=-1, keepdims=True) + jnp.exp(sink - mx)
                outs.append(jnp.dot(p.astype(BF16), vh, preferred_element_type=F32) / den)
            cols.append(jnp.where(lane_lo, outs[0], outs[1]))
        c_rows.append(jnp.concatenate(cols, axis=-1))
        pk_ref[r] = kkn[rows]
        pv_ref[r] = vv[rows]
    c_out = jnp.concatenate(c_rows, axis=0) if bb > 1 else c_rows[0]
    kk_ref[...] = jnp.where(lane_lo, kkn[:, :LANES], kkn[:, LANES:]).reshape(bb, CHUNK, LANES)
    vk_ref[...] = jnp.where(lane_lo, vv[:, :LANES], vv[:, LANES:]).reshape(bb, CHUNK, LANES)

    y = _merge(x, a_out, b_out, c_out, proj(O_G, D_INP),
               wpa_ref[...], wpb_ref[...], wpc_ref[...], wout_ref[...])
    y_ref[...] = y.reshape(bb, CHUNK, D_MODEL)


def _mixer_prompt(x, lw, bb=4):
    bsz, t, _ = x.shape
    nc = t // CHUNK
    assert t % CHUNK == 0 and bsz % bb == 0
    seq_blk = lambda w: pl.BlockSpec((bb, CHUNK, w), lambda i, c: (i, c, 0))
    keep_blk = lambda r, w: pl.BlockSpec((bb, r, w), lambda i, c: (i, 0, 0))
    consts = [lw['n1'], lw['win'], lw['cws_p'], lw['bs_p'], lw['lng'], lw['lnb'], lw['cw'], lw['cb'],
              lw['lam'], lw['wai'], lw['ba'], lw['bi'], lw['bdq'], lw['bdk'], lw['gq'], lw['gk'],
              lw['bias_p'], lw['wpa'], lw['wpb'], lw['wpc'], lw['wout']]
    out_shape = [jax.ShapeDtypeStruct((bsz, t, D_MODEL), F32),
                 jax.ShapeDtypeStruct((bsz, WINDOW, LANES), F32),
                 jax.ShapeDtypeStruct((bsz, WINDOW, LANES), F32),
                 jax.ShapeDtypeStruct((bsz, 8, D_LRU), F32),
                 jax.ShapeDtypeStruct((bsz, 1, D_LRU), F32)]
    return pl.pallas_call(
        functools.partial(_mixer_prompt_kernel, bb=bb),
        grid=(bsz // bb, nc),
        in_specs=[pl.BlockSpec(memory_space=pltpu.SMEM), seq_blk(D_MODEL)] + [_resident(a.shape) for a in consts],
        out_specs=[seq_blk(D_MODEL), keep_blk(WINDOW, LANES), keep_blk(WINDOW, LANES),
                   keep_blk(8, D_LRU), keep_blk(1, D_LRU)],
        out_shape=out_shape,
        scratch_shapes=[pltpu.VMEM((bb, CHUNK + 8, D_LRU), F32),
                        pltpu.VMEM((bb, 1, D_LRU), F32),
                        pltpu.VMEM((bb, CHUNK, 2 * LANES), F32),
                        pltpu.VMEM((bb, CHUNK, 2 * LANES), F32)],
        compiler_params=pltpu.CompilerParams(dimension_semantics=("parallel", "arbitrary"),
                                             vmem_limit_bytes=VMEM_LIMIT),
        name="mixer_prompt",
    )(lw['sinks'], x, *consts)


def _mixer_sample_kernel(x_ref, ck_ref, cv_ref, cs_ref, h0_ref, n1_ref, win_ref, cws_ref, bs_ref,
                         lng_ref, lnb_ref, cw_ref, cb_ref, lam_ref, wai_ref, ba_ref, bi_ref,
                         bdq_ref, bdk_ref, gq_ref, gk_ref, bias_ref, sink_ref,
                         wpa_ref, wpb_ref, wpc_ref, wout_ref,
                         y_ref, kk_ref, vk_ref, lx_ref, h_ref, cvn_ref,
                         xpad_ref, *, sb, t):
    m = sb * t

    x = x_ref[...].reshape(m, D_MODEL)
    xn = _rms(x, n1_ref[...]).astype(BF16)

    def proj(lo, hi):
        return jnp.dot(xn, win_ref[:, lo:hi], preferred_element_type=F32)

    ri = lax.broadcasted_iota(jnp.int32, (CHUNK, CHUNK), 0)
    ci = lax.broadcasted_iota(jnp.int32, (CHUNK, CHUNK), 1)
    keep = ((ri // t) == (ci // t)) & ((ci % t) <= (ri % t))
    wm = [jnp.where(keep, cws_ref[g], 0.0).astype(BF16) for g in range(N_CHUNK_GROUPS)]
    a_out, v_norm = _chunk_branch(proj(O_UV, O_LX), wm, bs_ref[...], lng_ref[...], lnb_ref[...])
    cvn_ref[...] = v_norm.reshape(sb, t, D_CHUNK)

    lx = proj(O_LX, O_LG)
    xpad_ref[:, 0:8, :] = cs_ref[...]
    xpad_ref[:, 8:8 + t, :] = lx.reshape(sb, t, D_LRU)
    taps = [lx] + [xpad_ref[:, 8 - d:8 - d + t, :].reshape(m, D_LRU) for d in range(1, CONV_W)]
    xc = _conv(taps, cw_ref[...], cb_ref[...])
    a, b = _lru_inputs(xc, lam_ref[...], wai_ref[...], ba_ref[...], bi_ref[...])
    a, b = _seg_scan(a, b, t)
    h0 = jnp.broadcast_to(h0_ref[...], (sb, t, D_LRU)).reshape(m, D_LRU)
    hs = a * h0 + b
    b_out = _gelu(proj(O_LG, O_Q)) * hs
    lx_ref[...] = lx.reshape(sb, t, D_LRU)
    h_ref[...] = hs.reshape(sb, t, D_LRU)

    qn = _head_norm(proj(O_Q, O_KK), bdq_ref[...], gq_ref[...]) * (HEAD_DIM ** -0.5)
    kkn = _head_norm(proj(O_KK, O_VV), bdk_ref[...], gk_ref[...])
    vv = proj(O_VV, O_G)
    lane = lax.broadcasted_iota(jnp.int32, (1, 1, LANES), 2)
    lane_lo = lane < HEAD_DIM
    lane_lo2 = lane_lo.reshape(1, LANES)
    knew = jnp.where(lane_lo2, kkn[:, :LANES], kkn[:, LANES:]).reshape(sb, t, LANES)
    vnew = jnp.where(lane_lo2, vv[:, :LANES], vv[:, LANES:]).reshape(sb, t, LANES)
    kc = jnp.concatenate([ck_ref[...], knew], axis=1)
    vc = jnp.concatenate([cv_ref[...], vnew], axis=1)
    kk_ref[...] = kc[:, t:, :]
    vk_ref[...] = vc[:, t:, :]
    q3 = qn.reshape(sb, t, D_ATTN)
    q3r = pltpu.roll(qn, HEAD_DIM, 1).reshape(sb, t, D_ATTN)
    pieces = []
    for hd in range(N_HEADS):
        kv = hd // (N_HEADS // N_KV_HEADS)
        if hd % 2 == kv:
            src, col = q3, hd // 2
        else:
            src, col = q3r, (hd + 1) // 2
        msk = lane_lo if kv == 0 else jnp.logical_not(lane_lo)
        pieces.append(jnp.where(msk, src[:, :, col * LANES:(col + 1) * LANES], 0.0))
    qs = jnp.concatenate(pieces, axis=1).astype(BF16)
    lg = jnp.einsum('bqd,bkd->bqk', qs, kc.astype(BF16), preferred_element_type=F32)
    lg = lg + bias_ref[...]
    sink = sink_ref[:, 0:1]
    mx = jnp.maximum(jnp.max(lg, axis=-1, keepdims=True), sink)
    p = jnp.exp(lg - mx)
    den = jnp.sum(p, axis=-1, keepdims=True) + jnp.exp(sink - mx)
    o = jnp.einsum('bqk,bkd->bqd', p.astype(BF16), vc.astype(BF16), preferred_element_type=F32) / den
    o_r = pltpu.roll(o.reshape(sb * N_HEADS * t, LANES), HEAD_DIM, 1).reshape(sb, N_HEADS * t, LANES)
    cols = []
    for j in range(N_HEADS // 2):
        kv = j // 2
        lo_src = o if kv == 0 else o_r
        hi_src = o_r if kv == 0 else o
        cols.append(jnp.where(lane_lo, lo_src[:, 2 * j * t:(2 * j + 1) * t, :],
                              hi_src[:, (2 * j + 1) * t:(2 * j + 2) * t, :]))
    c_out = jnp.concatenate(cols, axis=-1).reshape(m, D_ATTN)

    y = _merge(x, a_out, b_out, c_out, proj(O_G, D_INP),
               wpa_ref[...], wpb_ref[...], wpc_ref[...], wout_ref[...])
    y_ref[...] = y.reshape(sb, t, D_MODEL)


def _mixer_sample(x, ck, cv, cs, h0, lw, sb=32):
    nb, t, _ = x.shape
    assert nb % sb == 0 and t == 8 and (sb * t) % CHUNK == 0
    blk = lambda r, w: pl.BlockSpec((sb, r, w), lambda i: (i, 0, 0))
    consts = [lw['n1'], lw['win'], lw['cws_s'], lw['bs_s'], lw['lng'], lw['lnb'], lw['cw'], lw['cb'],
              lw['lam'], lw['wai'], lw['ba'], lw['bi'], lw['bdq'], lw['bdk'], lw['gq'], lw['gk'],
              lw['bias_s'], lw['sink_s'], lw['wpa'], lw['wpb'], lw['wpc'], lw['wout']]
    out_shape = [jax.ShapeDtypeStruct((nb, t, D_MODEL), F32),
                 jax.ShapeDtypeStruct((nb, WINDOW, LANES), F32),
                 jax.ShapeDtypeStruct((nb, WINDOW, LANES), F32),
                 jax.ShapeDtypeStruct((nb, t, D_LRU), F32),
                 jax.ShapeDtypeStruct((nb, t, D_LRU), F32),
                 jax.ShapeDtypeStruct((nb, t, D_CHUNK), F32)]
    return pl.pallas_call(
        functools.partial(_mixer_sample_kernel, sb=sb, t=t),
        grid=(nb // sb,),
        in_specs=[blk(t, D_MODEL), blk(WINDOW, LANES), blk(WINDOW, LANES), blk(8, D_LRU), blk(1, D_LRU)]
                 + [_resident(a.shape) for a in consts],
        out_specs=[blk(t, D_MODEL), blk(WINDOW, LANES), blk(WINDOW, LANES),
                   blk(t, D_LRU), blk(t, D_LRU), blk(t, D_CHUNK)],
        out_shape=out_shape,
        scratch_shapes=[pltpu.VMEM((sb, 8 + t, D_LRU), F32)],
        compiler_params=pltpu.CompilerParams(dimension_semantics=("parallel",),
                                             vmem_limit_bytes=VMEM_LIMIT),
        name="mixer_sample",
    )(x, ck, cv, cs, h0, *consts)


def _ffn_kernel(x_ref, g_ref, wg_ref, wu_ref, wd_ref, o_ref, xn_ref):
    f = pl.program_id(1)

    @pl.when(f == 0)
    def _():
        x = x_ref[...]
        xn_ref[...] = _rms(x, g_ref[...]).astype(BF16)
        o_ref[...] = x

    xn = xn_ref[...]
    gate = jnp.dot(xn, wg_ref[...], preferred_element_type=F32)
    up = jnp.dot(xn, wu_ref[...], preferred_element_type=F32)
    o_ref[...] += _mm(gate * jax.nn.sigmoid(gate) * up, wd_ref[...])


def _ffn(x, g, wg, wu, wd, tm=1024, tf=896):
    n = x.shape[0]
    tm = min(tm, n)
    assert n % tm == 0 and D_FF % tf == 0
    return pl.pallas_call(
        _ffn_kernel,
        grid=(n // tm, D_FF // tf),
        in_specs=[pl.BlockSpec((tm, D_MODEL), lambda i, f: (i, 0)),
                  _resident(g.shape),
                  pl.BlockSpec((D_MODEL, tf), lambda i, f: (0, f)),
                  pl.BlockSpec((D_MODEL, tf), lambda i, f: (0, f)),
                  pl.BlockSpec((tf, D_MODEL), lambda i, f: (f, 0))],
        out_specs=pl.BlockSpec((tm, D_MODEL), lambda i, f: (i, 0)),
        out_shape=jax.ShapeDtypeStruct((n, D_MODEL), F32),
        scratch_shapes=[pltpu.VMEM((tm, D_MODEL), BF16)],
        compiler_params=pltpu.CompilerParams(dimension_semantics=("parallel", "arbitrary"),
                                             vmem_limit_bytes=VMEM_LIMIT),
        name="ffn_dense",
    )(x, g, wg, wu, wd)


def _router_kernel(x_ref, g_ref, wr_ref, tri_ref, xn_ref, comb_ref, rank_ref, combt_ref, rankt_ref, cnt_ref):
    xn = _rms(x_ref[...], g_ref[...])
    xn_ref[...] = xn.astype(BF16)
    wr = wr_ref[...]
    x_hi = xn.astype(BF16)
    x_lo = (xn - x_hi.astype(F32)).astype(BF16)
    w_hi = wr.astype(BF16)
    w_lo = (wr - w_hi.astype(F32)).astype(BF16)
    logits = (jnp.dot(x_hi, w_hi, preferred_element_type=F32) + jnp.dot(x_lo, w_hi, preferred_element_type=F32)
              + jnp.dot(x_hi, w_lo, preferred_element_type=F32))
    lane = lax.broadcasted_iota(jnp.int32, logits.shape, 1).astype(F32)
    logits = jnp.where(lane < N_EXPERTS, logits, NEG)
    m1 = jnp.max(logits, axis=-1, keepdims=True)
    i1 = jnp.min(jnp.where(logits == m1, lane, float(LANES)), axis=-1, keepdims=True)
    rest = jnp.where(lane == i1, NEG, logits)
    m2 = jnp.max(rest, axis=-1, keepdims=True)
    i2 = jnp.min(jnp.where(rest == m2, lane, float(LANES)), axis=-1, keepdims=True)
    e2 = jnp.exp(m2 - m1)
    w1 = 1.0 / (1.0 + e2)
    comb = jnp.where(lane == i1, w1, 0.0) + jnp.where(lane == i2, e2 * w1, 0.0)
    sel = jnp.where(comb > 0.0, 1.0, 0.0)
    nsub = x_ref.shape[0] // MOE_SUB_TOKENS
    selb = sel.astype(BF16)
    ranks, cnts = [], []
    for h in range(nsub):
        rows = slice(h * MOE_SUB_TOKENS, (h + 1) * MOE_SUB_TOKENS)
        ranks.append(jnp.dot(tri_ref[...], selb[rows], preferred_element_type=F32))
        cnts.append(jnp.sum(sel[rows], axis=0, keepdims=True))
    rank = jnp.concatenate(ranks, axis=0) if nsub > 1 else ranks[0]
    cnts = cnts + [jnp.zeros((8 - nsub, LANES), F32)]
    comb_ref[...] = comb
    rank_ref[...] = rank
    combt_ref[...] = comb.T[:N_EXPERTS]
    rankt_ref[...] = rank.T[:N_EXPERTS]
    cnt_ref[...] = jnp.concatenate(cnts, axis=0)[None]


def _router(x, g, wr, tri, tb):
    n = x.shape[0]
    nblk = n // tb
    row = lambda w: pl.BlockSpec((tb, w), lambda i: (i, 0))
    colb = pl.BlockSpec((N_EXPERTS, tb), lambda i: (0, i))
    return pl.pallas_call(
        _router_kernel,
        grid=(nblk,),
        in_specs=[row(D_MODEL), _resident(g.shape), _resident(wr.shape), _resident(tri.shape)],
        out_specs=[row(D_MODEL), row(LANES), row(LANES), colb, colb,
                   pl.BlockSpec((1, 8, LANES), lambda i: (i, 0, 0))],
        out_shape=[jax.ShapeDtypeStruct((n, D_MODEL), BF16),
                   jax.ShapeDtypeStruct((n, LANES), F32),
                   jax.ShapeDtypeStruct((n, LANES), F32),
                   jax.ShapeDtypeStruct((N_EXPERTS, n), F32),
                   jax.ShapeDtypeStruct((N_EXPERTS, n), F32),
                   jax.ShapeDtypeStruct((nblk, 8, LANES), F32)],
        compiler_params=pltpu.CompilerParams(dimension_semantics=("parallel",),
                                             vmem_limit_bytes=VMEM_LIMIT),
        name="moe_router",
    )(x, g, wr, tri)


def _moe_kernel(meta_ref, xn_ref, comb_ref, rank_ref, combt_ref, rankt_ref, wg_ref, wu_ref, wd_ref,
                yin_ref, yout_ref, *, tb):
    e = meta_ref[0]
    blk = pl.program_id(0)
    nsub = tb // MOE_SUB_TOKENS
    nmax = jnp.int32(0)
    for h in range(nsub):
        nmax = jnp.maximum(nmax, meta_ref[1 + blk * nsub + h])

    lane_e = lax.broadcasted_iota(jnp.int32, (tb, LANES), 1) == e
    cw_col = jnp.sum(jnp.where(lane_e, comb_ref[...], 0.0), axis=-1, keepdims=True)
    rk_col = jnp.sum(jnp.where(lane_e, rank_ref[...], 0.0), axis=-1, keepdims=True)
    sub_e = lax.broadcasted_iota(jnp.int32, (N_EXPERTS, tb), 0) == e
    cw_row = jnp.sum(jnp.where(sub_e, combt_ref[...], 0.0), axis=0, keepdims=True)
    rk_row = jnp.sum(jnp.where(sub_e, rankt_ref[...], 0.0), axis=0, keepdims=True)
    rk_col = jnp.where(cw_col > 0.0, rk_col, -1.0)
    rk_row = jnp.where(cw_row > 0.0, rk_row, -1.0)

    def tile(cap, row0, src_ref):
        j0 = row0.astype(F32)
        sub_i = lax.broadcasted_iota(jnp.int32, (cap, MOE_SUB_TOKENS), 0).astype(F32)
        lane_i = lax.broadcasted_iota(jnp.int32, (MOE_SUB_TOKENS, cap), 1).astype(F32)
        xg = []
        for h in range(nsub):
            tok = slice(h * MOE_SUB_TOKENS, (h + 1) * MOE_SUB_TOKENS)
            gather = jnp.where(rk_row[:, tok] - j0 == sub_i, 1.0, 0.0).astype(BF16)
            xg.append(jnp.dot(gather, xn_ref[tok, :], preferred_element_type=F32))
        xg = (jnp.concatenate(xg, axis=0) if nsub > 1 else xg[0]).astype(BF16)
        gate = jnp.dot(xg, wg_ref[0], preferred_element_type=F32)
        up = jnp.dot(xg, wu_ref[0], preferred_element_type=F32)
        o = _mm(gate * jax.nn.sigmoid(gate) * up, wd_ref[0]).astype(BF16)
        for h in range(nsub):
            tok = slice(h * MOE_SUB_TOKENS, (h + 1) * MOE_SUB_TOKENS)
            scatter = jnp.where(rk_col[tok] - j0 == lane_i, 1.0, 0.0).astype(BF16)
            part = cw_col[tok] * jnp.dot(scatter, o[h * cap:(h + 1) * cap], preferred_element_type=F32)
            yout_ref[tok, :] = src_ref[tok, :] + part

    step = MOE_PASS_ROWS[1] - MOE_PASS_ROWS[0]
    cls = jnp.clip(lax.div(nmax - MOE_PASS_ROWS[0] + (step - 1), jnp.int32(step)), 0, len(MOE_PASS_ROWS) - 1)
    first = [functools.partial(tile, cap, jnp.int32(0), yin_ref) for cap in MOE_PASS_ROWS]
    lax.switch(cls, first)
    big, small = MOE_PASS_ROWS[-1], MOE_PASS_ROWS[0]
    nextra = lax.div(jnp.maximum(nmax - big, 0) + (small - 1), jnp.int32(small))

    def loop_body(j, carry):
        tile(small, big + j * small, yout_ref)
        return carry

    lax.fori_loop(0, nextra, loop_body, 0)


def _moe_expert(meta, xn, comb, rank, combt, rankt, wg, wu, wd, y, tb):
    n = xn.shape[0]
    row = lambda w: pl.BlockSpec((tb, w), lambda b, m: (b, 0))
    colb = pl.BlockSpec((N_EXPERTS, tb), lambda b, m: (0, b))
    wspec = lambda r, w: pl.BlockSpec((1, r, w), lambda b, m: (m[0], 0, 0), pipeline_mode=pl.Buffered(1))
    grid_spec = pltpu.PrefetchScalarGridSpec(
        num_scalar_prefetch=1,
        grid=(n // tb,),
        in_specs=[row(D_MODEL), row(LANES), row(LANES), colb, colb,
                  wspec(D_MODEL, D_FF), wspec(D_MODEL, D_FF), wspec(D_FF, D_MODEL), row(D_MODEL)],
        out_specs=row(D_MODEL),
    )
    return pl.pallas_call(
        functools.partial(_moe_kernel, tb=tb),
        grid_spec=grid_spec,
        out_shape=jax.ShapeDtypeStruct((n, D_MODEL), F32),
        compiler_params=pltpu.CompilerParams(dimension_semantics=("arbitrary",),
                                             vmem_limit_bytes=VMEM_LIMIT),
        name="moe_expert",
    )(meta, xn, comb, rank, combt, rankt, wg, wu, wd, y)


def _moe_ffn(x, g, wr, tri, wg, wu, wd, tb):
    xn, comb, rank, combt, rankt, cnt = _router(x, g, wr, tri, tb)
    nsub = tb // MOE_SUB_TOKENS
    cnt = jnp.transpose(cnt[:, :nsub, :N_EXPERTS].astype(jnp.int32), (2, 0, 1)).reshape(N_EXPERTS, -1)
    y = x
    for e in range(N_EXPERTS):
        meta = jnp.concatenate([jnp.full((1,), e, jnp.int32), cnt[e]])
        y = _moe_expert(meta, xn, comb, rank, combt, rankt, wg, wu, wd, y, tb)
    return y


def _block_diag_mean(n, width):
    i = jnp.arange(n) // width
    return jnp.where(i[:, None] == i[None, :], 1.0 / width, 0.0).astype(BF16)


def _alibi_bias(nq, nk):
    dist = (WINDOW + jnp.arange(nq)[:, None] - jnp.arange(nk)[None, :]).astype(F32)
    allowed = (dist >= 0) & (dist <= WINDOW)
    slopes = jnp.exp2(-8.0 * (jnp.arange(N_HEADS, dtype=F32) + 1.0) / N_HEADS)
    return jnp.where(allowed[None], -slopes[:, None, None] * dist[None], NEG)


def _pack_layer(l, p, t_s):
    w_in = p['w_in'][l]
    dup = lambda w: jnp.concatenate([w[:, i * HEAD_DIM:(i + 1) * HEAD_DIM] for i in range(N_KV_HEADS) for _ in (0, 1)], axis=1)
    s_q, s_k = 2 * D_CHUNK + 2 * D_LRU, 2 * D_CHUNK + 2 * D_LRU + D_ATTN
    s_v = s_k + N_KV_HEADS * HEAD_DIM
    s_g = s_v + N_KV_HEADS * HEAD_DIM
    win = jnp.concatenate([w_in[:, :s_k], dup(w_in[:, s_k:s_v]), dup(w_in[:, s_v:s_g]), w_in[:, s_g:]], axis=1)
    assert win.shape[1] == D_INP
    nblk = D_LRU // p['lru_wa'].shape[-1]
    bd = lambda w: jax.scipy.linalg.block_diag(*[w[i] for i in range(nblk)])
    gw = D_CHUNK // N_CHUNK_GROUPS
    rep = CHUNK // t_s
    bias_s = _alibi_bias(t_s, WINDOW + t_s).reshape(N_HEADS * t_s, WINDOW + t_s)
    return {
        'n1': p['norm1_g'][l][None], 'win': win.astype(BF16),
        'cws_p': p['chunk_ws'][l],
        'cws_s': jnp.tile(p['chunk_ws'][l][:, :t_s, :t_s], (1, rep, rep)),
        'bs_p': jnp.repeat(p['chunk_bs'][l].T, gw, axis=1),
        'bs_s': jnp.tile(jnp.repeat(p['chunk_bs'][l][:, :t_s].T, gw, axis=1), (rep, 1)),
        'lng': p['chunk_ln_g'][l][None], 'lnb': p['chunk_ln_b'][l][None],
        'cw': p['conv_w'][l], 'cb': p['conv_b'][l][None], 'lam': p['lru_lambda'][l][None],
        'wai': jnp.concatenate([bd(p['lru_wa'][l]), bd(p['lru_wi'][l])], axis=1).astype(BF16),
        'ba': p['lru_ba'][l][None], 'bi': p['lru_bi'][l][None],
        'bdq': _block_diag_mean(D_ATTN, HEAD_DIM), 'bdk': _block_diag_mean(2 * LANES, LANES),
        'gq': jnp.tile(p['q_norm_g'][l], N_HEADS)[None],
        'gk': jnp.tile(p['k_norm_g'][l], 2 * N_KV_HEADS)[None],
        'bias_p': _alibi_bias(CHUNK, 2 * WINDOW), 'bias_s': bias_s,
        'sinks': p['attn_sinks'][l],
        'sink_s': jnp.broadcast_to(jnp.repeat(p['attn_sinks'][l], t_s)[:, None], (N_HEADS * t_s, LANES)),
        'wpa': p['w_proj_a'][l].astype(BF16), 'wpb': p['w_proj_b'][l].astype(BF16),
        'wpc': p['w_proj_c'][l].astype(BF16), 'wout': p['w_out'][l].astype(BF16),
    }


def _strict_lower(n):
    i = jnp.arange(n)
    return jnp.where(i[None, :] < i[:, None], 1.0, 0.0).astype(BF16)


def kernel(x_prompt, x_sample, cache_win_k, cache_win_v, state_conv, state_lru_h, norm1_g, w_in, chunk_ln_g, chunk_ln_b, chunk_ws, chunk_bs, conv_w, conv_b, lru_lambda, lru_wa, lru_ba, lru_wi, lru_bi, q_norm_g, k_norm_g, attn_sinks, w_proj_a, w_proj_b, w_proj_c, w_out, norm2_g, ffn_w_gate, ffn_w_up, ffn_w_down, moe_router, moe_w_gate, moe_w_up, moe_w_down):
    p = dict(norm1_g=norm1_g, w_in=w_in, chunk_ln_g=chunk_ln_g, chunk_ln_b=chunk_ln_b, chunk_ws=chunk_ws,
             chunk_bs=chunk_bs, conv_w=conv_w, conv_b=conv_b, lru_lambda=lru_lambda, lru_wa=lru_wa,
             lru_ba=lru_ba, lru_wi=lru_wi, lru_bi=lru_bi, q_norm_g=q_norm_g, k_norm_g=k_norm_g,
             attn_sinks=attn_sinks, w_proj_a=w_proj_a, w_proj_b=w_proj_b, w_proj_c=w_proj_c, w_out=w_out)
    depth = w_in.shape[0]
    bp, tp, _ = x_prompt.shape
    bs, ts, _ = x_sample.shape
    tb_p = min(1024, bp * tp)
    tb_s = min(1024, bs * ts)
    assert tb_p % MOE_SUB_TOKENS == 0 and tb_s % MOE_SUB_TOKENS == 0
    tri = _strict_lower(MOE_SUB_TOKENS)

    yp, ys = x_prompt, x_sample
    outs = {k: [] for k in ('pk', 'pv', 'pconv', 'ph', 'sk', 'sv', 'sconv', 'sh', 'schunk')}
    for l in range(depth):
        lw = _pack_layer(l, p, ts)
        yp, pk, pv, plx, phl = _mixer_prompt(yp, lw)
        ck = cache_win_k[l].reshape(bs, WINDOW, LANES)
        cv = cache_win_v[l].reshape(bs, WINDOW, LANES)
        cs = jnp.pad(state_conv[l], ((0, 0), (8 - (CONV_W - 1), 0), (0, 0)))
        ys, sk, sv, slx, shl, scv = _mixer_sample(ys, ck, cv, cs, state_lru_h[l][:, None, :], lw)
        outs['pk'].append(pk.reshape(bp, WINDOW, N_KV_HEADS, HEAD_DIM))
        outs['pv'].append(pv.reshape(bp, WINDOW, N_KV_HEADS, HEAD_DIM))
        outs['pconv'].append(plx[:, 8 - (CONV_W - 1):, :])
        outs['ph'].append(phl[:, 0, :])
        outs['sk'].append(sk.reshape(bs, WINDOW, N_KV_HEADS, HEAD_DIM))
        outs['sv'].append(sv.reshape(bs, WINDOW, N_KV_HEADS, HEAD_DIM))
        outs['sconv'].append(slx[:, ts - (CONV_W - 1):, :])
        outs['sh'].append(shl[:, ts - 1, :])
        outs['schunk'].append(scv)

        g2 = norm2_g[l][None]
        flat_p, flat_s = yp.reshape(bp * tp, D_MODEL), ys.reshape(bs * ts, D_MODEL)
        j = l // 2
        if l % 2 == 0:
            wg, wu, wd = ffn_w_gate[j].astype(BF16), ffn_w_up[j].astype(BF16), ffn_w_down[j].astype(BF16)
            flat_p = _ffn(flat_p, g2, wg, wu, wd)
            flat_s = _ffn(flat_s, g2, wg, wu, wd)
        else:
            wg, wu, wd = moe_w_gate[j].astype(BF16), moe_w_up[j].astype(BF16), moe_w_down[j].astype(BF16)
            wr = jnp.pad(moe_router[j], ((0, 0), (0, LANES - N_EXPERTS)))
            flat_p = _moe_ffn(flat_p, g2, wr, tri, wg, wu, wd, tb_p)
            flat_s = _moe_ffn(flat_s, g2, wr, tri, wg, wu, wd, tb_s)
        yp, ys = flat_p.reshape(bp, tp, D_MODEL), flat_s.reshape(bs, ts, D_MODEL)

    st = {k: jnp.stack(v) for k, v in outs.items()}
    return (yp, ys, st['pk'], st['pv'], st['pconv'], st['ph'],
            st['sk'], st['sv'], st['sconv'], st['sh'], st['schunk'])
```

```python
import functools
import math

import jax
import jax.numpy as jnp
from jax import lax
from jax.experimental import pallas as pl
from jax.experimental.pallas import tpu as pltpu

F32 = jnp.float32
BF16 = jnp.bfloat16

D_MODEL = 1024
D_CHUNK = 512
CHUNK = 128
N_CHUNK_GROUPS = 4
D_LRU = 512
CONV_W = 4
LRU_C = 8.0
N_HEADS = 8
N_KV_HEADS = 2
HEAD_DIM = 64
D_ATTN = N_HEADS * HEAD_DIM
WINDOW = 128
D_FF = 3584
N_EXPERTS = 8
EPS = 1e-6
NEG = -1e30

O_UV = 0
O_LX = 2 * D_CHUNK
O_LG = O_LX + D_LRU
O_Q = O_LG + D_LRU
O_KK = O_Q + D_ATTN
O_VV = O_KK + 2 * N_KV_HEADS * HEAD_DIM
O_G = O_VV + 2 * N_KV_HEADS * HEAD_DIM
D_INP = O_G + 3 * D_MODEL

LANES = 128
MOE_SUB_TOKENS = 512
MOE_PASS_ROWS = (96, 144, 192)
VMEM_LIMIT = 56 * 1024 * 1024


def _resident(shape):
    nd = len(shape)
    return pl.BlockSpec(shape, lambda *_: (0,) * nd, pipeline_mode=pl.Buffered(1))


def _mm(a, w):
    return jnp.dot(a.astype(BF16), w, preferred_element_type=F32)


def _gelu(x):
    return 0.5 * x * (1.0 + jnp.tanh(0.7978845608028654 * (x + 0.044715 * (x * x * x))))


def _rms(x, g):
    return x * lax.rsqrt(jnp.mean(x * x, axis=-1, keepdims=True) + EPS) * g


def _seg_scan(a, b, seg):
    pos = lax.broadcasted_iota(jnp.int32, (a.shape[0], 1), 0) & (seg - 1)
    s = 1
    while s < seg:
        ok = pos >= s
        a_sh = pltpu.roll(a, s, 0)
        b_sh = pltpu.roll(b, s, 0)
        b = jnp.where(ok, a * b_sh + b, b)
        a = jnp.where(ok, a * a_sh, a)
        s *= 2
    return a, b


def _chunk_branch(uv, wm, bs_full, ln_g, ln_b):
    uv = _gelu(uv)
    u = uv[:, :D_CHUNK]
    v = uv[:, D_CHUNK:]
    mu = jnp.mean(v, axis=-1, keepdims=True)
    vc = v - mu
    v = vc * lax.rsqrt(jnp.mean(vc * vc, axis=-1, keepdims=True) + EPS) * ln_g + ln_b
    vb = v.astype(BF16)
    nblk = uv.shape[0] // CHUNK
    gw = D_CHUNK // N_CHUNK_GROUPS
    rows = []
    for r in range(nblk):
        cols = []
        for g in range(N_CHUNK_GROUPS):
            vg = vb[r * CHUNK:(r + 1) * CHUNK, g * gw:(g + 1) * gw]
            cols.append(jnp.dot(wm[g], vg, preferred_element_type=F32))
        rows.append(jnp.concatenate(cols, axis=-1) + bs_full)
    s = jnp.concatenate(rows, axis=0) if nblk > 1 else rows[0]
    return u * s, v


def _lru_inputs(xc, lam, w_ai, b_a, b_i):
    ri = _mm(xc, w_ai)
    r = jax.nn.sigmoid(ri[:, :D_LRU] + b_a)
    i = jax.nn.sigmoid(ri[:, D_LRU:] + b_i)
    log_sig = jnp.minimum(lam, 0.0) - jnp.log1p(jnp.exp(-jnp.abs(lam)))
    log_a = LRU_C * r * log_sig
    a = jnp.exp(log_a)
    t = jnp.tanh(log_a)
    one_m_a2 = -2.0 * t / (1.0 - t)
    return a, jnp.sqrt(one_m_a2) * (i * xc)


def _conv(taps, conv_w, conv_b):
    out = conv_b + conv_w[3:4] * taps[0]
    for d in range(1, CONV_W):
        out = out + conv_w[3 - d:4 - d] * taps[d]
    return out


def _head_norm(x, bd, g):
    ms = jnp.dot((x * x).astype(BF16), bd, preferred_element_type=F32)
    return x * lax.rsqrt(ms + EPS) * g


def _merge(x, a_out, b_out, c_out, gl, wpa, wpb, wpc, wout):
    gates = jax.nn.sigmoid(gl)
    merged = (gates[:, :D_MODEL] * _mm(a_out, wpa)
              + gates[:, D_MODEL:2 * D_MODEL] * _mm(b_out, wpb)
              + gates[:, 2 * D_MODEL:] * _mm(c_out, wpc))
    return x + _mm(merged, wout)


def _mixer_prompt_kernel(sinks_ref, x_ref, n1_ref, win_ref, cws_ref, bs_ref, lng_ref, lnb_ref,
                         cw_ref, cb_ref, lam_ref, wai_ref, ba_ref, bi_ref, bdq_ref, bdk_ref,
                         gq_ref, gk_ref, bias_ref, wpa_ref, wpb_ref, wpc_ref, wout_ref,
                         y_ref, kk_ref, vk_ref, lx_ref, h_ref,
                         xpad_ref, hc_ref, pk_ref, pv_ref, *, bb):
    c = pl.program_id(1)
    m = bb * CHUNK

    @pl.when(c == 0)
    def _():
        xpad_ref[:, 0:8, :] = jnp.zeros((bb, 8, D_LRU), F32)
        hc_ref[...] = jnp.zeros_like(hc_ref)
        pk_ref[...] = jnp.zeros_like(pk_ref)
        pv_ref[...] = jnp.zeros_like(pv_ref)

    x = x_ref[...].reshape(m, D_MODEL)
    xn = _rms(x, n1_ref[...]).astype(BF16)

    def proj(lo, hi):
        return jnp.dot(xn, win_ref[:, lo:hi], preferred_element_type=F32)

    ri = lax.broadcasted_iota(jnp.int32, (CHUNK, CHUNK), 0)
    ci = lax.broadcasted_iota(jnp.int32, (CHUNK, CHUNK), 1)
    wm = [jnp.where(ri >= ci, cws_ref[g], 0.0).astype(BF16) for g in range(N_CHUNK_GROUPS)]
    a_out, _ = _chunk_branch(proj(O_UV, O_LX), wm, bs_ref[...], lng_ref[...], lnb_ref[...])

    lx = proj(O_LX, O_LG)
    xpad_ref[:, 8:8 + CHUNK, :] = lx.reshape(bb, CHUNK, D_LRU)
    taps = [lx] + [xpad_ref[:, 8 - d:8 - d + CHUNK, :].reshape(m, D_LRU) for d in range(1, CONV_W)]
    xc = _conv(taps, cw_ref[...], cb_ref[...])
    a, b = _lru_inputs(xc, lam_ref[...], wai_ref[...], ba_ref[...], bi_ref[...])
    a, b = _seg_scan(a, b, CHUNK)
    hs = []
    for r in range(bb):
        rows = slice(r * CHUNK, (r + 1) * CHUNK)
        h_r = a[rows] * hc_ref[r] + b[rows]
        hc_ref[r] = h_r[CHUNK - 1:CHUNK]
        hs.append(h_r)
    hs = jnp.concatenate(hs, axis=0) if bb > 1 else hs[0]
    b_out = _gelu(proj(O_LG, O_Q)) * hs
    tail = xpad_ref[:, CHUNK:CHUNK + 8, :]
    xpad_ref[:, 0:8, :] = tail
    lx_ref[...] = tail
    h_ref[...] = hc_ref[...]

    qn = _head_norm(proj(O_Q, O_KK), bdq_ref[...], gq_ref[...]) * (HEAD_DIM ** -0.5)
    kkn = _head_norm(proj(O_KK, O_VV), bdk_ref[...], gk_ref[...])
    vv = proj(O_VV, O_G)
    lane = lax.broadcasted_iota(jnp.int32, (1, LANES), 1)
    lane_lo = lane < HEAD_DIM
    kcol = lax.broadcasted_iota(jnp.int32, (1, 2 * WINDOW), 1)
    pen = jnp.where(kcol < WINDOW, jnp.where(c == 0, NEG, 0.0), 0.0)
    c_rows = []
    for r in range(bb):
        rows = slice(r * CHUNK, (r + 1) * CHUNK)
        kcat = jnp.concatenate([pk_ref[r], kkn[rows]], axis=0).astype(BF16)
        vcat = jnp.concatenate([pv_ref[r], vv[rows]], axis=0).astype(BF16)
        cols = []
        for j in range(N_HEADS // 2):
            kv = j // 2
            qj = qn[rows, j * LANES:(j + 1) * LANES]
            kh = kcat[:, kv * LANES:(kv + 1) * LANES]
            vh = vcat[:, kv * LANES:(kv + 1) * LANES]
            outs = []
            for half in range(2):
                hd = 2 * j + half
                qm = jnp.where(lane_lo if half == 0 else jnp.logical_not(lane_lo), qj, 0.0).astype(BF16)
                lg = lax.dot_general(qm, kh, (((1,), (1,)), ((), ())), preferred_element_type=F32)
                lg = lg + bias_ref[hd] + pen
                sink = sinks_ref[hd]
                mx = jnp.maximum(jnp.max(lg, axis=-1, keepdims=True), sink)
                p = jnp.exp(lg - mx)
                den = jnp.sum(p, axis=-1, keepdims=True) + jnp.exp(sink - mx)
                outs.append(jnp.dot(p.astype(BF16), vh, preferred_element_type=F32) / den)
            cols.append(jnp.where(lane_lo, outs[0], outs[1]))
        c_rows.append(jnp.concatenate(cols, axis=-1))
        pk_ref[r] = kkn[rows]
        pv_ref[r] = vv[rows]
    c_out = jnp.concatenate(c_rows, axis=0) if bb > 1 else c_rows[0]
    kk_ref[...] = jnp.where(lane_lo, kkn[:, :LANES], kkn[:, LANES:]).reshape(bb, CHUNK, LANES)
    vk_ref[...] = jnp.where(lane_lo, vv[:, :LANES], vv[:, LANES:]).reshape(bb, CHUNK, LANES)

    y = _merge(x, a_out, b_out, c_out, proj(O_G, D_INP),
               wpa_ref[...], wpb_ref[...], wpc_ref[...], wout_ref[...])
    y_ref[...] = y.reshape(bb, CHUNK, D_MODEL)


def _mixer_prompt(x, lw, bb=4):
    bsz, t, _ = x.shape
    nc = t // CHUNK
    assert t % CHUNK == 0 and bsz % bb == 0
    seq_blk = lambda w: pl.BlockSpec((bb, CHUNK, w), lambda i, c: (i, c, 0))
    keep_blk = lambda r, w: pl.BlockSpec((bb, r, w), lambda i, c: (i, 0, 0))
    consts = [lw['n1'], lw['win'], lw['cws_p'], lw['bs_p'], lw['lng'], lw['lnb'], lw['cw'], lw['cb'],
              lw['lam'], lw['wai'], lw['ba'], lw['bi'], lw['bdq'], lw['bdk'], lw['gq'], lw['gk'],
              lw['bias_p'], lw['wpa'], lw['wpb'], lw['wpc'], lw['wout']]
    out_shape = [jax.ShapeDtypeStruct((bsz, t, D_MODEL), F32),
                 jax.ShapeDtypeStruct((bsz, WINDOW, LANES), F32),
                 jax.ShapeDtypeStruct((bsz, WINDOW, LANES), F32),
                 jax.ShapeDtypeStruct((bsz, 8, D_LRU), F32),
                 jax.ShapeDtypeStruct((bsz, 1, D_LRU), F32)]
    return pl.pallas_call(
        functools.partial(_mixer_prompt_kernel, bb=bb),
        grid=(bsz // bb, nc),
        in_specs=[pl.BlockSpec(memory_space=pltpu.SMEM), seq_blk(D_MODEL)] + [_resident(a.shape) for a in consts],
        out_specs=[seq_blk(D_MODEL), keep_blk(WINDOW, LANES), keep_blk(WINDOW, LANES),
                   keep_blk(8, D_LRU), keep_blk(1, D_LRU)],
        out_shape=out_shape,
        scratch_shapes=[pltpu.VMEM((bb, CHUNK + 8, D_LRU), F32),
                        pltpu.VMEM((bb, 1, D_LRU), F32),
                        pltpu.VMEM((bb, CHUNK, 2 * LANES), F32),
                        pltpu.VMEM((bb, CHUNK, 2 * LANES), F32)],
        compiler_params=pltpu.CompilerParams(dimension_semantics=("parallel", "arbitrary"),
                                             vmem_limit_bytes=VMEM_LIMIT),
        name="mixer_prompt",
    )(lw['sinks'], x, *consts)


def _mixer_sample_kernel(x_ref, ck_ref, cv_ref, cs_ref, h0_ref, n1_ref, win_ref, cws_ref, bs_ref,
                         lng_ref, lnb_ref, cw_ref, cb_ref, lam_ref, wai_ref, ba_ref, bi_ref,
                         bdq_ref, bdk_ref, gq_ref, gk_ref, bias_ref, sink_ref,
                         wpa_ref, wpb_ref, wpc_ref, wout_ref,
                         y_ref, kk_ref, vk_ref, lx_ref, h_ref, cvn_ref,
                         xpad_ref, *, sb, t):
    m = sb * t

    x = x_ref[...].reshape(m, D_MODEL)
    xn = _rms(x, n1_ref[...]).astype(BF16)

    def proj(lo, hi):
        return jnp.dot(xn, win_ref[:, lo:hi], preferred_element_type=F32)

    ri = lax.broadcasted_iota(jnp.int32, (CHUNK, CHUNK), 0)
    ci = lax.broadcasted_iota(jnp.int32, (CHUNK, CHUNK), 1)
    keep = ((ri // t) == (ci // t)) & ((ci % t) <= (ri % t))
    wm = [jnp.where(keep, cws_ref[g], 0.0).astype(BF16) for g in range(N_CHUNK_GROUPS)]
    a_out, v_norm = _chunk_branch(proj(O_UV, O_LX), wm, bs_ref[...], lng_ref[...], lnb_ref[...])
    cvn_ref[...] = v_norm.reshape(sb, t, D_CHUNK)

    lx = proj(O_LX, O_LG)
    xpad_ref[:, 0:8, :] = cs_ref[...]
    xpad_ref[:, 8:8 + t, :] = lx.reshape(sb, t, D_LRU)
    taps = [lx] + [xpad_ref[:, 8 - d:8 - d + t, :].reshape(m, D_LRU) for d in range(1, CONV_W)]
    xc = _conv(taps, cw_ref[...], cb_ref[...])
    a, b = _lru_inputs(xc, lam_ref[...], wai_ref[...], ba_ref[...], bi_ref[...])
    a, b = _seg_scan(a, b, t)
    h0 = jnp.broadcast_to(h0_ref[...], (sb, t, D_LRU)).reshape(m, D_LRU)
    hs = a * h0 + b
    b_out = _gelu(proj(O_LG, O_Q)) * hs
    lx_ref[...] = lx.reshape(sb, t, D_LRU)
    h_ref[...] = hs.reshape(sb, t, D_LRU)

    qn = _head_norm(proj(O_Q, O_KK), bdq_ref[...], gq_ref[...]) * (HEAD_DIM ** -0.5)
    kkn = _head_norm(proj(O_KK, O_VV), bdk_ref[...], gk_ref[...])
    vv = proj(O_VV, O_G)
    lane = lax.broadcasted_iota(jnp.int32, (1, 1, LANES), 2)
    lane_lo = lane < HEAD_DIM
    lane_lo2 = lane_lo.reshape(1, LANES)
    knew = jnp.where(lane_lo2, kkn[:, :LANES], kkn[:, LANES:]).reshape(sb, t, LANES)
    vnew = jnp.where(lane_lo2, vv[:, :LANES], vv[:, LANES:]).reshape(sb, t, LANES)
    kc = jnp.concatenate([ck_ref[...], knew], axis=1)
    vc = jnp.concatenate([cv_ref[...], vnew], axis=1)
    kk_ref[...] = kc[:, t:, :]
    vk_ref[...] = vc[:, t:, :]
    q3 = qn.reshape(sb, t, D_ATTN)
    q3r = pltpu.roll(qn, HEAD_DIM, 1).reshape(sb, t, D_ATTN)
    pieces = []
    for hd in range(N_HEADS):
        kv = hd // (N_HEADS // N_KV_HEADS)
        if hd % 2 == kv:
            src, col = q3, hd // 2
        else:
            src, col = q3r, (hd + 1) // 2
        msk = lane_lo if kv == 0 else jnp.logical_not(lane_lo)
        pieces.append(jnp.where(msk, src[:, :, col * LANES:(col + 1) * LANES], 0.0))
    qs = jnp.concatenate(pieces, axis=1).astype(BF16)
    lg = jnp.einsum('bqd,bkd->bqk', qs, kc.astype(BF16), preferred_element_type=F32)
    lg = lg + bias_ref[...]
    sink = sink_ref[:, 0:1]
    mx = jnp.maximum(jnp.max(lg, axis=-1, keepdims=True), sink)
    p = jnp.exp(lg - mx)
    den = jnp.sum(p, axis=-1, keepdims=True) + jnp.exp(sink - mx)
    o = jnp.einsum('bqk,bkd->bqd', p.astype(BF16), vc.astype(BF16), preferred_element_type=F32) / den
    o_r = pltpu.roll(o.reshape(sb * N_HEADS * t, LANES), HEAD_DIM, 1).reshape(sb, N_HEADS * t, LANES)
    cols = []
    for j in range(N_HEADS // 2):
        kv = j // 2
        lo_src = o if kv == 0 else o_r
        hi_src = o_r if kv == 0 else o
        cols.append(jnp.where(lane_lo, lo_src[:, 2 * j * t:(2 * j + 1) * t, :],
                              hi_src[:, (2 * j + 1) * t:(2 * j + 2) * t, :]))
    c_out = jnp.concatenate(cols, axis=-1).reshape(m, D_ATTN)

    y = _merge(x, a_out, b_out, c_out, proj(O_G, D_INP),
               wpa_ref[...], wpb_ref[...], wpc_ref[...], wout_ref[...])
    y_ref[...] = y.reshape(sb, t, D_MODEL)


def _mixer_sample(x, ck, cv, layer, cs, h0, lw, sb=32):
    nb, t, _ = x.shape
    assert nb % sb == 0 and t == 8 and (sb * t) % CHUNK == 0
    blk = lambda r, w: pl.BlockSpec((sb, r, w), lambda i: (i, 0, 0))
    cache_blk = pl.BlockSpec((sb, WINDOW, LANES), lambda i: (layer * (nb // sb) + i, 0, 0))
    consts = [lw['n1'], lw['win'], lw['cws_s'], lw['bs_s'], lw['lng'], lw['lnb'], lw['cw'], lw['cb'],
              lw['lam'], lw['wai'], lw['ba'], lw['bi'], lw['bdq'], lw['bdk'], lw['gq'], lw['gk'],
              lw['bias_s'], lw['sink_s'], lw['wpa'], lw['wpb'], lw['wpc'], lw['wout']]
    out_shape = [jax.ShapeDtypeStruct((nb, t, D_MODEL), F32),
                 jax.ShapeDtypeStruct((nb, WINDOW, LANES), F32),
                 jax.ShapeDtypeStruct((nb, WINDOW, LANES), F32),
                 jax.ShapeDtypeStruct((nb, t, D_LRU), F32),
                 jax.ShapeDtypeStruct((nb, t, D_LRU), F32),
                 jax.ShapeDtypeStruct((nb, t, D_CHUNK), F32)]
    return pl.pallas_call(
        functools.partial(_mixer_sample_kernel, sb=sb, t=t),
        grid=(nb // sb,),
        in_specs=[blk(t, D_MODEL), cache_blk, cache_blk, blk(8, D_LRU), blk(1, D_LRU)]
                 + [_resident(a.shape) for a in consts],
        out_specs=[blk(t, D_MODEL), blk(WINDOW, LANES), blk(WINDOW, LANES),
                   blk(t, D_LRU), blk(t, D_LRU), blk(t, D_CHUNK)],
        out_shape=out_shape,
        scratch_shapes=[pltpu.VMEM((sb, 8 + t, D_LRU), F32)],
        compiler_params=pltpu.CompilerParams(dimension_semantics=("parallel",),
                                             vmem_limit_bytes=VMEM_LIMIT),
        name="mixer_sample",
    )(x, ck, cv, cs, h0, *consts)


def _ffn_kernel(x_ref, g_ref, wg_ref, wu_ref, wd_ref, o_ref, *, tf):
    x = x_ref[...]
    xn = _rms(x, g_ref[...]).astype(BF16)
    acc = x
    for f in range(D_FF // tf):
        cols = slice(f * tf, (f + 1) * tf)
        gate = jnp.dot(xn, wg_ref[:, cols], preferred_element_type=F32)
        up = jnp.dot(xn, wu_ref[:, cols], preferred_element_type=F32)
        acc = acc + _mm(gate * jax.nn.sigmoid(gate) * up, wd_ref[cols, :])
    o_ref[...] = acc


def _ffn(x, g, wg, wu, wd, tm=512, tf=512):
    n = x.shape[0]
    tm = min(tm, n)
    assert n % tm == 0 and D_FF % tf == 0
    return pl.pallas_call(
        functools.partial(_ffn_kernel, tf=tf),
        grid=(n // tm,),
        in_specs=[pl.BlockSpec((tm, D_MODEL), lambda i: (i, 0)),
                  _resident(g.shape), _resident(wg.shape), _resident(wu.shape), _resident(wd.shape)],
        out_specs=pl.BlockSpec((tm, D_MODEL), lambda i: (i, 0)),
        out_shape=jax.ShapeDtypeStruct((n, D_MODEL), F32),
        compiler_params=pltpu.CompilerParams(dimension_semantics=("parallel",),
                                             vmem_limit_bytes=VMEM_LIMIT),
        name="ffn_dense",
    )(x, g, wg, wu, wd)


def _router_kernel(x_ref, g_ref, wr_ref, tri_ref, xn_ref, comb_ref, rank_ref, combt_ref, rankt_ref, cnt_ref):
    xn = _rms(x_ref[...], g_ref[...])
    xn_ref[...] = xn.astype(BF16)
    wr = wr_ref[...]
    x_hi = xn.astype(BF16)
    x_lo = (xn - x_hi.astype(F32)).astype(BF16)
    w_hi = wr.astype(BF16)
    w_lo = (wr - w_hi.astype(F32)).astype(BF16)
    logits = (jnp.dot(x_hi, w_hi, preferred_element_type=F32) + jnp.dot(x_lo, w_hi, preferred_element_type=F32)
              + jnp.dot(x_hi, w_lo, preferred_element_type=F32))
    lane = lax.broadcasted_iota(jnp.int32, logits.shape, 1).astype(F32)
    logits = jnp.where(lane < N_EXPERTS, logits, NEG)
    m1 = jnp.max(logits, axis=-1, keepdims=True)
    i1 = jnp.min(jnp.where(logits == m1, lane, float(LANES)), axis=-1, keepdims=True)
    rest = jnp.where(lane == i1, NEG, logits)
    m2 = jnp.max(rest, axis=-1, keepdims=True)
    i2 = jnp.min(jnp.where(rest == m2, lane, float(LANES)), axis=-1, keepdims=True)
    e2 = jnp.exp(m2 - m1)
    w1 = 1.0 / (1.0 + e2)
    comb = jnp.where(lane == i1, w1, 0.0) + jnp.where(lane == i2, e2 * w1, 0.0)
    sel = jnp.where(comb > 0.0, 1.0, 0.0)
    nsub = x_ref.shape[0] // MOE_SUB_TOKENS
    selb = sel.astype(BF16)
    ranks, cnts = [], []
    for h in range(nsub):
        rows = slice(h * MOE_SUB_TOKENS, (h + 1) * MOE_SUB_TOKENS)
        ranks.append(jnp.dot(tri_ref[...], selb[rows], preferred_element_type=F32))
        cnts.append(jnp.sum(sel[rows], axis=0, keepdims=True))
    rank = jnp.concatenate(ranks, axis=0) if nsub > 1 else ranks[0]
    cnts = cnts + [jnp.zeros((8 - nsub, LANES), F32)]
    comb_ref[...] = comb
    rank_ref[...] = rank
    combt_ref[...] = comb.T[:N_EXPERTS]
    rankt_ref[...] = rank.T[:N_EXPERTS]
    cnt_ref[...] = jnp.concatenate(cnts, axis=0)[None]


def _router(x, g, wr, tri, tb):
    n = x.shape[0]
    nblk = n // tb
    row = lambda w: pl.BlockSpec((tb, w), lambda i: (i, 0))
    colb = pl.BlockSpec((N_EXPERTS, tb), lambda i: (0, i))
    return pl.pallas_call(
        _router_kernel,
        grid=(nblk,),
        in_specs=[row(D_MODEL), _resident(g.shape), _resident(wr.shape), _resident(tri.shape)],
        out_specs=[row(D_MODEL), row(LANES), row(LANES), colb, colb,
                   pl.BlockSpec((1, 8, LANES), lambda i: (i, 0, 0))],
        out_shape=[jax.ShapeDtypeStruct((n, D_MODEL), BF16),
                   jax.ShapeDtypeStruct((n, LANES), F32),
                   jax.ShapeDtypeStruct((n, LANES), F32),
                   jax.ShapeDtypeStruct((N_EXPERTS, n), F32),
                   jax.ShapeDtypeStruct((N_EXPERTS, n), F32),
                   jax.ShapeDtypeStruct((nblk, 8, LANES), F32)],
        compiler_params=pltpu.CompilerParams(dimension_semantics=("parallel",),
                                             vmem_limit_bytes=VMEM_LIMIT),
        name="moe_router",
    )(x, g, wr, tri)


def _moe_kernel(meta_ref, xn_ref, comb_ref, rank_ref, combt_ref, rankt_ref, wg_ref, wu_ref, wd_ref,
                yin_ref, yout_ref, *, tb, resident):
    nsub = tb // MOE_SUB_TOKENS
    if resident:
        e = pl.program_id(0)
        cnt0 = 1 + e * nsub
    else:
        e = meta_ref[0]
        cnt0 = 1 + pl.program_id(0) * nsub
    nmax = jnp.int32(0)
    for h in range(nsub):
        nmax = jnp.maximum(nmax, meta_ref[cnt0 + h])

    lane_e = lax.broadcasted_iota(jnp.int32, (tb, LANES), 1) == e
    cw_col = jnp.sum(jnp.where(lane_e, comb_ref[...], 0.0), axis=-1, keepdims=True)
    rk_col = jnp.sum(jnp.where(lane_e, rank_ref[...], 0.0), axis=-1, keepdims=True)
    sub_e = lax.broadcasted_iota(jnp.int32, (N_EXPERTS, tb), 0) == e
    cw_row = jnp.sum(jnp.where(sub_e, combt_ref[...], 0.0), axis=0, keepdims=True)
    rk_row = jnp.sum(jnp.where(sub_e, rankt_ref[...], 0.0), axis=0, keepdims=True)
    rk_col = jnp.where(cw_col > 0.0, rk_col, -1.0)
    rk_row = jnp.where(cw_row > 0.0, rk_row, -1.0)

    def tile(cap, row0, src_ref):
        j0 = row0.astype(F32)
        sub_i = lax.broadcasted_iota(jnp.int32, (cap, MOE_SUB_TOKENS), 0).astype(F32)
        lane_i = lax.broadcasted_iota(jnp.int32, (MOE_SUB_TOKENS, cap), 1).astype(F32)
        xg = []
        for h in range(nsub):
            tok = slice(h * MOE_SUB_TOKENS, (h + 1) * MOE_SUB_TOKENS)
            gather = jnp.where(rk_row[:, tok] - j0 == sub_i, 1.0, 0.0).astype(BF16)
            xg.append(jnp.dot(gather, xn_ref[tok, :], preferred_element_type=F32))
        xg = (jnp.concatenate(xg, axis=0) if nsub > 1 else xg[0]).astype(BF16)
        gate = jnp.dot(xg, wg_ref[0], preferred_element_type=F32)
        up = jnp.dot(xg, wu_ref[0], preferred_element_type=F32)
        o = _mm(gate * jax.nn.sigmoid(gate) * up, wd_ref[0]).astype(BF16)
        for h in range(nsub):
            tok = slice(h * MOE_SUB_TOKENS, (h + 1) * MOE_SUB_TOKENS)
            scatter = jnp.where(rk_col[tok] - j0 == lane_i, 1.0, 0.0).astype(BF16)
            part = cw_col[tok] * jnp.dot(scatter, o[h * cap:(h + 1) * cap], preferred_element_type=F32)
            yout_ref[tok, :] = src_ref[tok, :] + part

    step = MOE_PASS_ROWS[1] - MOE_PASS_ROWS[0]
    cls = jnp.clip(lax.div(nmax - MOE_PASS_ROWS[0] + (step - 1), jnp.int32(step)), 0, len(MOE_PASS_ROWS) - 1)
    if resident:
        @pl.when(e == 0)
        def _():
            yout_ref[...] = yin_ref[...]
    first_src = yout_ref if resident else yin_ref
    lax.switch(cls, [functools.partial(tile, cap, jnp.int32(0), first_src) for cap in MOE_PASS_ROWS])
    big, small = MOE_PASS_ROWS[-1], MOE_PASS_ROWS[0]
    nextra = lax.div(jnp.maximum(nmax - big, 0) + (small - 1), jnp.int32(small))

    def loop_body(j, carry):
        tile(small, big + j * small, yout_ref)
        return carry

    lax.fori_loop(0, nextra, loop_body, 0)


def _moe_expert(meta, xn, comb, rank, combt, rankt, wg, wu, wd, y, tb, resident):
    n = xn.shape[0]
    if resident:
        assert n == tb
        grid = (N_EXPERTS,)
        tok_idx = lambda i, m: 0
        exp_idx = lambda i, m: i
    else:
        grid = (n // tb,)
        tok_idx = lambda i, m: i
        exp_idx = lambda i, m: m[0]
    row = lambda w: pl.BlockSpec((tb, w), lambda i, m: (tok_idx(i, m), 0))
    colb = pl.BlockSpec((N_EXPERTS, tb), lambda i, m: (0, tok_idx(i, m)))
    wspec = lambda r, w: pl.BlockSpec((1, r, w), lambda i, m: (exp_idx(i, m), 0, 0), pipeline_mode=pl.Buffered(1))
    grid_spec = pltpu.PrefetchScalarGridSpec(
        num_scalar_prefetch=1,
        grid=grid,
        in_specs=[row(D_MODEL), row(LANES), row(LANES), colb, colb,
                  wspec(D_MODEL, D_FF), wspec(D_MODEL, D_FF), wspec(D_FF, D_MODEL), row(D_MODEL)],
        out_specs=row(D_MODEL),
    )
    return pl.pallas_call(
        functools.partial(_moe_kernel, tb=tb, resident=resident),
        grid_spec=grid_spec,
        out_shape=jax.ShapeDtypeStruct((n, D_MODEL), F32),
        compiler_params=pltpu.CompilerParams(dimension_semantics=("arbitrary",),
                                             vmem_limit_bytes=VMEM_LIMIT),
        name="moe_resident" if resident else "moe_expert",
    )(meta, xn, comb, rank, combt, rankt, wg, wu, wd, y)


def _moe_ffn(x, g, wr, tri, wg, wu, wd, tb):
    xn, comb, rank, combt, rankt, cnt = _router(x, g, wr, tri, tb)
    nsub = tb // MOE_SUB_TOKENS
    cnt = jnp.transpose(cnt[:, :nsub, :N_EXPERTS].astype(jnp.int32), (2, 0, 1)).reshape(N_EXPERTS, -1)
    zero = jnp.zeros((1,), jnp.int32)
    if x.shape[0] == tb:
        return _moe_expert(jnp.concatenate([zero, cnt.reshape(-1)]), xn, comb, rank, combt, rankt,
                           wg, wu, wd, x, tb, True)
    y = x
    for e in range(N_EXPERTS):
        y = _moe_expert(jnp.concatenate([zero + e, cnt[e]]), xn, comb, rank, combt, rankt, wg, wu, wd, y, tb, False)
    return y


def _block_diag_mean(n, width):
    i = jnp.arange(n) // width
    return jnp.where(i[:, None] == i[None, :], 1.0 / width, 0.0).astype(BF16)


def _alibi_bias(nq, nk):
    dist = (WINDOW + jnp.arange(nq)[:, None] - jnp.arange(nk)[None, :]).astype(F32)
    allowed = (dist >= 0) & (dist <= WINDOW)
    slopes = jnp.exp2(-8.0 * (jnp.arange(N_HEADS, dtype=F32) + 1.0) / N_HEADS)
    return jnp.where(allowed[None], -slopes[:, None, None] * dist[None], NEG)


def _pack_layer(l, p, t_s):
    w_in = p['w_in'][l].astype(BF16)
    dup = lambda w: jnp.concatenate([w[:, i * HEAD_DIM:(i + 1) * HEAD_DIM] for i in range(N_KV_HEADS) for _ in (0, 1)], axis=1)
    s_q, s_k = 2 * D_CHUNK + 2 * D_LRU, 2 * D_CHUNK + 2 * D_LRU + D_ATTN
    s_v = s_k + N_KV_HEADS * HEAD_DIM
    s_g = s_v + N_KV_HEADS * HEAD_DIM
    win = jnp.concatenate([w_in[:, :s_k], dup(w_in[:, s_k:s_v]), dup(w_in[:, s_v:s_g]), w_in[:, s_g:]], axis=1)
    assert win.shape[1] == D_INP
    nblk = D_LRU // p['lru_wa'].shape[-1]
    bd = lambda w: jax.scipy.linalg.block_diag(*[w[i] for i in range(nblk)])
    gw = D_CHUNK // N_CHUNK_GROUPS
    rep = CHUNK // t_s
    bias_s = _alibi_bias(t_s, WINDOW + t_s).reshape(N_HEADS * t_s, WINDOW + t_s)
    return {
        'n1': p['norm1_g'][l][None], 'win': win,
        'cws_p': p['chunk_ws'][l],
        'cws_s': jnp.tile(p['chunk_ws'][l][:, :t_s, :t_s], (1, rep, rep)),
        'bs_p': jnp.repeat(p['chunk_bs'][l].T, gw, axis=1),
        'bs_s': jnp.tile(jnp.repeat(p['chunk_bs'][l][:, :t_s].T, gw, axis=1), (rep, 1)),
        'lng': p['chunk_ln_g'][l][None], 'lnb': p['chunk_ln_b'][l][None],
        'cw': p['conv_w'][l], 'cb': p['conv_b'][l][None], 'lam': p['lru_lambda'][l][None],
        'wai': jnp.concatenate([bd(p['lru_wa'][l]), bd(p['lru_wi'][l])], axis=1).astype(BF16),
        'ba': p['lru_ba'][l][None], 'bi': p['lru_bi'][l][None],
        'bdq': _block_diag_mean(D_ATTN, HEAD_DIM), 'bdk': _block_diag_mean(2 * LANES, LANES),
        'gq': jnp.tile(p['q_norm_g'][l], N_HEADS)[None],
        'gk': jnp.tile(p['k_norm_g'][l], 2 * N_KV_HEADS)[None],
        'bias_p': _alibi_bias(CHUNK, 2 * WINDOW), 'bias_s': bias_s,
        'sinks': p['attn_sinks'][l],
        'sink_s': jnp.broadcast_to(jnp.repeat(p['attn_sinks'][l], t_s)[:, None], (N_HEADS * t_s, LANES)),
        'wpa': p['w_proj_a'][l].astype(BF16), 'wpb': p['w_proj_b'][l].astype(BF16),
        'wpc': p['w_proj_c'][l].astype(BF16), 'wout': p['w_out'][l].astype(BF16),
    }


def _strict_lower(n):
    i = jnp.arange(n)
    return jnp.where(i[None, :] < i[:, None], 1.0, 0.0).astype(BF16)


def kernel(x_prompt, x_sample, cache_win_k, cache_win_v, state_conv, state_lru_h, norm1_g, w_in, chunk_ln_g, chunk_ln_b, chunk_ws, chunk_bs, conv_w, conv_b, lru_lambda, lru_wa, lru_ba, lru_wi, lru_bi, q_norm_g, k_norm_g, attn_sinks, w_proj_a, w_proj_b, w_proj_c, w_out, norm2_g, ffn_w_gate, ffn_w_up, ffn_w_down, moe_router, moe_w_gate, moe_w_up, moe_w_down):
    p = dict(norm1_g=norm1_g, w_in=w_in, chunk_ln_g=chunk_ln_g, chunk_ln_b=chunk_ln_b, chunk_ws=chunk_ws,
             chunk_bs=chunk_bs, conv_w=conv_w, conv_b=conv_b, lru_lambda=lru_lambda, lru_wa=lru_wa,
             lru_ba=lru_ba, lru_wi=lru_wi, lru_bi=lru_bi, q_norm_g=q_norm_g, k_norm_g=k_norm_g,
             attn_sinks=attn_sinks, w_proj_a=w_proj_a, w_proj_b=w_proj_b, w_proj_c=w_proj_c, w_out=w_out)
    depth = w_in.shape[0]
    bp, tp, _ = x_prompt.shape
    bs, ts, _ = x_sample.shape
    tb_p = min(1024, bp * tp)
    tb_s = min(1024, bs * ts)
    assert tb_p % MOE_SUB_TOKENS == 0 and tb_s % MOE_SUB_TOKENS == 0
    tri = _strict_lower(MOE_SUB_TOKENS)

    yp, ys = x_prompt, x_sample
    ck = cache_win_k.reshape(depth * bs, WINDOW, LANES)
    cv = cache_win_v.reshape(depth * bs, WINDOW, LANES)
    outs = {k: [] for k in ('pk', 'pv', 'pconv', 'ph', 'sk', 'sv', 'sconv', 'sh', 'schunk')}
    for l in range(depth):
        lw = _pack_layer(l, p, ts)
        yp, pk, pv, plx, phl = _mixer_prompt(yp, lw)
        cs = jnp.pad(state_conv[l], ((0, 0), (8 - (CONV_W - 1), 0), (0, 0)))
        ys, sk, sv, slx, shl, scv = _mixer_sample(ys, ck, cv, l, cs, state_lru_h[l][:, None, :], lw)
        outs['pk'].append(pk.reshape(bp, WINDOW, N_KV_HEADS, HEAD_DIM))
        outs['pv'].append(pv.reshape(bp, WINDOW, N_KV_HEADS, HEAD_DIM))
        outs['pconv'].append(plx[:, 8 - (CONV_W - 1):, :])
        outs['ph'].append(phl[:, 0, :])
        outs['sk'].append(sk.reshape(bs, WINDOW, N_KV_HEADS, HEAD_DIM))
        outs['sv'].append(sv.reshape(bs, WINDOW, N_KV_HEADS, HEAD_DIM))
        outs['sconv'].append(slx[:, ts - (CONV_W - 1):, :])
        outs['sh'].append(shl[:, ts - 1, :])
        outs['schunk'].append(scv)

        g2 = norm2_g[l][None]
        flat_p, flat_s = yp.reshape(bp * tp, D_MODEL), ys.reshape(bs * ts, D_MODEL)
        j = l // 2
        if l % 2 == 0:
            wg, wu, wd = ffn_w_gate[j].astype(BF16), ffn_w_up[j].astype(BF16), ffn_w_down[j].astype(BF16)
            flat_p = _ffn(flat_p, g2, wg, wu, wd)
            flat_s = _ffn(flat_s, g2, wg, wu, wd)
        else:
            wg, wu, wd = moe_w_gate[j].astype(BF16), moe_w_up[j].astype(BF16), moe_w_down[j].astype(BF16)
            wr = jnp.pad(moe_router[j], ((0, 0), (0, LANES - N_EXPERTS)))
            flat_p = _moe_ffn(flat_p, g2, wr, tri, wg, wu, wd, tb_p)
            flat_s = _moe_ffn(flat_s, g2, wr, tri, wg, wu, wd, tb_s)
        yp, ys = flat_p.reshape(bp, tp, D_MODEL), flat_s.reshape(bs, ts, D_MODEL)

    st = {k: jnp.stack(v) for k, v in outs.items()}
    return (yp, ys, st['pk'], st['pv'], st['pconv'], st['ph'],
            st['sk'], st['sv'], st['sconv'], st['sh'], st['schunk'])
```

```python
import functools
import math

import jax
import jax.numpy as jnp
from jax import lax
from jax.experimental import pallas as pl
from jax.experimental.pallas import tpu as pltpu

F32 = jnp.float32
BF16 = jnp.bfloat16

D_MODEL = 1024
D_CHUNK = 512
CHUNK = 128
N_CHUNK_GROUPS = 4
D_LRU = 512
CONV_W = 4
LRU_C = 8.0
N_HEADS = 8
N_KV_HEADS = 2
HEAD_DIM = 64
D_ATTN = N_HEADS * HEAD_DIM
WINDOW = 128
D_FF = 3584
N_EXPERTS = 8
EPS = 1e-6
NEG = -1e30

O_UV = 0
O_LX = 2 * D_CHUNK
O_LG = O_LX + D_LRU
O_Q = O_LG + D_LRU
O_KK = O_Q + D_ATTN
O_VV = O_KK + 2 * N_KV_HEADS * HEAD_DIM
O_G = O_VV + 2 * N_KV_HEADS * HEAD_DIM
D_INP = O_G + 3 * D_MODEL

LANES = 128
MOE_SUB_TOKENS = 512
MOE_PASS_ROWS = (96, 128, 160, 192)
MOE_EXTRA_ROWS = 48
VMEM_LIMIT = 56 * 1024 * 1024


def _resident(shape):
    nd = len(shape)
    return pl.BlockSpec(shape, lambda *_: (0,) * nd, pipeline_mode=pl.Buffered(1))


def _mm(a, w):
    return jnp.dot(a.astype(BF16), w, preferred_element_type=F32)


def _gelu(x):
    return 0.5 * x * (1.0 + jnp.tanh(0.7978845608028654 * (x + 0.044715 * (x * x * x))))


def _rms(x, g):
    return x * lax.rsqrt(jnp.mean(x * x, axis=-1, keepdims=True) + EPS) * g


def _seg_scan(a, b, seg):
    pos = lax.broadcasted_iota(jnp.int32, (a.shape[0], 1), 0) & (seg - 1)
    s = 1
    while s < seg:
        ok = pos >= s
        a_sh = pltpu.roll(a, s, 0)
        b_sh = pltpu.roll(b, s, 0)
        b = jnp.where(ok, a * b_sh + b, b)
        a = jnp.where(ok, a * a_sh, a)
        s *= 2
    return a, b


def _chunk_branch(uv, wm, bs_full, ln_g, ln_b):
    uv = _gelu(uv)
    u = uv[:, :D_CHUNK]
    v = uv[:, D_CHUNK:]
    mu = jnp.mean(v, axis=-1, keepdims=True)
    vc = v - mu
    v = vc * lax.rsqrt(jnp.mean(vc * vc, axis=-1, keepdims=True) + EPS) * ln_g + ln_b
    vb = v.astype(BF16)
    nblk = uv.shape[0] // CHUNK
    gw = D_CHUNK // N_CHUNK_GROUPS
    rows = []
    for r in range(nblk):
        cols = []
        for g in range(N_CHUNK_GROUPS):
            vg = vb[r * CHUNK:(r + 1) * CHUNK, g * gw:(g + 1) * gw]
            cols.append(jnp.dot(wm[g], vg, preferred_element_type=F32))
        rows.append(jnp.concatenate(cols, axis=-1) + bs_full)
    s = jnp.concatenate(rows, axis=0) if nblk > 1 else rows[0]
    return u * s, v


def _lru_inputs(xc, lam, w_ai, b_a, b_i):
    ri = _mm(xc, w_ai)
    r = jax.nn.sigmoid(ri[:, :D_LRU] + b_a)
    i = jax.nn.sigmoid(ri[:, D_LRU:] + b_i)
    log_sig = jnp.minimum(lam, 0.0) - jnp.log1p(jnp.exp(-jnp.abs(lam)))
    log_a = LRU_C * r * log_sig
    a = jnp.exp(log_a)
    t = jnp.tanh(log_a)
    one_m_a2 = -2.0 * t / (1.0 - t)
    return a, jnp.sqrt(one_m_a2) * (i * xc)


def _conv(taps, conv_w, conv_b):
    out = conv_b + conv_w[3:4] * taps[0]
    for d in range(1, CONV_W):
        out = out + conv_w[3 - d:4 - d] * taps[d]
    return out


def _head_norm(x, bd, g):
    ms = jnp.dot((x * x).astype(BF16), bd, preferred_element_type=F32)
    return x * lax.rsqrt(ms + EPS) * g


def _merge(x, a_out, b_out, c_out, gl, wpa, wpb, wpc, wout):
    gates = jax.nn.sigmoid(gl)
    merged = (gates[:, :D_MODEL] * _mm(a_out, wpa)
              + gates[:, D_MODEL:2 * D_MODEL] * _mm(b_out, wpb)
              + gates[:, 2 * D_MODEL:] * _mm(c_out, wpc))
    return x + _mm(merged, wout)


def _mixer_prompt_kernel(sinks_ref, x_ref, n1_ref, win_ref, cws_ref, bs_ref, lng_ref, lnb_ref,
                         cw_ref, cb_ref, lam_ref, wai_ref, ba_ref, bi_ref, bdq_ref, bdk_ref,
                         gq_ref, gk_ref, bias_ref, wpa_ref, wpb_ref, wpc_ref, wout_ref,
                         y_ref, kk_ref, vk_ref, lx_ref, h_ref,
                         xpad_ref, hc_ref, pk_ref, pv_ref, *, bb):
    c = pl.program_id(1)
    m = bb * CHUNK

    @pl.when(c == 0)
    def _():
        xpad_ref[:, 0:8, :] = jnp.zeros((bb, 8, D_LRU), F32)
        hc_ref[...] = jnp.zeros_like(hc_ref)
        pk_ref[...] = jnp.zeros_like(pk_ref)
        pv_ref[...] = jnp.zeros_like(pv_ref)

    x = x_ref[...].reshape(m, D_MODEL)
    xn = _rms(x, n1_ref[...]).astype(BF16)

    def proj(lo, hi):
        return jnp.dot(xn, win_ref[:, lo:hi], preferred_element_type=F32)

    ri = lax.broadcasted_iota(jnp.int32, (CHUNK, CHUNK), 0)
    ci = lax.broadcasted_iota(jnp.int32, (CHUNK, CHUNK), 1)
    wm = [jnp.where(ri >= ci, cws_ref[g], 0.0).astype(BF16) for g in range(N_CHUNK_GROUPS)]
    a_out, _ = _chunk_branch(proj(O_UV, O_LX), wm, bs_ref[...], lng_ref[...], lnb_ref[...])

    lx = proj(O_LX, O_LG)
    xpad_ref[:, 8:8 + CHUNK, :] = lx.reshape(bb, CHUNK, D_LRU)
    taps = [lx] + [xpad_ref[:, 8 - d:8 - d + CHUNK, :].reshape(m, D_LRU) for d in range(1, CONV_W)]
    xc = _conv(taps, cw_ref[...], cb_ref[...])
    a, b = _lru_inputs(xc, lam_ref[...], wai_ref[...], ba_ref[...], bi_ref[...])
    a, b = _seg_scan(a, b, CHUNK)
    hs = []
    for r in range(bb):
        rows = slice(r * CHUNK, (r + 1) * CHUNK)
        h_r = a[rows] * hc_ref[r] + b[rows]
        hc_ref[r] = h_r[CHUNK - 1:CHUNK]
        hs.append(h_r)
    hs = jnp.concatenate(hs, axis=0) if bb > 1 else hs[0]
    b_out = _gelu(proj(O_LG, O_Q)) * hs
    tail = xpad_ref[:, CHUNK:CHUNK + 8, :]
    xpad_ref[:, 0:8, :] = tail
    lx_ref[...] = tail
    h_ref[...] = hc_ref[...]

    qn = _head_norm(proj(O_Q, O_KK), bdq_ref[...], gq_ref[...]) * (HEAD_DIM ** -0.5)
    kkn = _head_norm(proj(O_KK, O_VV), bdk_ref[...], gk_ref[...])
    vv = proj(O_VV, O_G)
    lane = lax.broadcasted_iota(jnp.int32, (1, LANES), 1)
    lane_lo = lane < HEAD_DIM
    kcol = lax.broadcasted_iota(jnp.int32, (1, 2 * WINDOW), 1)
    pen = jnp.where(kcol < WINDOW, jnp.where(c == 0, NEG, 0.0), 0.0)
    c_rows = []
    for r in range(bb):
        rows = slice(r * CHUNK, (r + 1) * CHUNK)
        kcat = jnp.concatenate([pk_ref[r], kkn[rows]], axis=0).astype(BF16)
        vcat = jnp.concatenate([pv_ref[r], vv[rows]], axis=0).astype(BF16)
        cols = []
        for j in range(N_HEADS // 2):
            kv = j // 2
            qj = qn[rows, j * LANES:(j + 1) * LANES]
            kh = kcat[:, kv * LANES:(kv + 1) * LANES]
            vh = vcat[:, kv * LANES:(kv + 1) * LANES]
            outs = []
            for half in range(2):
                hd = 2 * j + half
                qm = jnp.where(lane_lo if half == 0 else jnp.logical_not(lane_lo), qj, 0.0).astype(BF16)
                lg = lax.dot_general(qm, kh, (((1,), (1,)), ((), ())), preferred_element_type=F32)
                lg = lg + bias_ref[hd] + pen
                sink = sinks_ref[hd]
                mx = jnp.maximum(jnp.max(lg, axis=-1, keepdims=True), sink)
                p = jnp.exp(lg - mx)
                den = jnp.sum(p, axis=-1, keepdims=True) + jnp.exp(sink - mx)
                outs.append(jnp.dot(p.astype(BF16), vh, preferred_element_type=F32) / den)
            cols.append(jnp.where(lane_lo, outs[0], outs[1]))
        c_rows.append(jnp.concatenate(cols, axis=-1))
        pk_ref[r] = kkn[rows]
        pv_ref[r] = vv[rows]
    c_out = jnp.concatenate(c_rows, axis=0) if bb > 1 else c_rows[0]
    kk_ref[...] = jnp.where(lane_lo, kkn[:, :LANES], kkn[:, LANES:]).reshape(bb, CHUNK, LANES)
    vk_ref[...] = jnp.where(lane_lo, vv[:, :LANES], vv[:, LANES:]).reshape(bb, CHUNK, LANES)

    y = _merge(x, a_out, b_out, c_out, proj(O_G, D_INP),
               wpa_ref[...], wpb_ref[...], wpc_ref[...], wout_ref[...])
    y_ref[...] = y.reshape(bb, CHUNK, D_MODEL)


def _mixer_prompt(x, lw, bb=4):
    bsz, t, _ = x.shape
    nc = t // CHUNK
    assert t % CHUNK == 0 and bsz % bb == 0
    seq_blk = lambda w: pl.BlockSpec((bb, CHUNK, w), lambda i, c: (i, c, 0))
    keep_blk = lambda r, w: pl.BlockSpec((bb, r, w), lambda i, c: (i, 0, 0))
    consts = [lw['n1'], lw['win'], lw['cws_p'], lw['bs_p'], lw['lng'], lw['lnb'], lw['cw'], lw['cb'],
              lw['lam'], lw['wai'], lw['ba'], lw['bi'], lw['bdq'], lw['bdk'], lw['gq'], lw['gk'],
              lw['bias_p'], lw['wpa'], lw['wpb'], lw['wpc'], lw['wout']]
    out_shape = [jax.ShapeDtypeStruct((bsz, t, D_MODEL), F32),
                 jax.ShapeDtypeStruct((bsz, WINDOW, LANES), F32),
                 jax.ShapeDtypeStruct((bsz, WINDOW, LANES), F32),
                 jax.ShapeDtypeStruct((bsz, 8, D_LRU), F32),
                 jax.ShapeDtypeStruct((bsz, 1, D_LRU), F32)]
    return pl.pallas_call(
        functools.partial(_mixer_prompt_kernel, bb=bb),
        grid=(bsz // bb, nc),
        in_specs=[pl.BlockSpec(memory_space=pltpu.SMEM), seq_blk(D_MODEL)] + [_resident(a.shape) for a in consts],
        out_specs=[seq_blk(D_MODEL), keep_blk(WINDOW, LANES), keep_blk(WINDOW, LANES),
                   keep_blk(8, D_LRU), keep_blk(1, D_LRU)],
        out_shape=out_shape,
        scratch_shapes=[pltpu.VMEM((bb, CHUNK + 8, D_LRU), F32),
                        pltpu.VMEM((bb, 1, D_LRU), F32),
                        pltpu.VMEM((bb, CHUNK, 2 * LANES), F32),
                        pltpu.VMEM((bb, CHUNK, 2 * LANES), F32)],
        compiler_params=pltpu.CompilerParams(dimension_semantics=("parallel", "arbitrary"),
                                             vmem_limit_bytes=VMEM_LIMIT),
        name="mixer_prompt",
    )(lw['sinks'], x, *consts)


def _mixer_sample_kernel(x_ref, ck_ref, cv_ref, cs_ref, h0_ref, n1_ref, win_ref, cws_ref, bs_ref,
                         lng_ref, lnb_ref, cw_ref, cb_ref, lam_ref, wai_ref, ba_ref, bi_ref,
                         bdq_ref, bdk_ref, gq_ref, gk_ref, bias_ref, sink_ref,
                         wpa_ref, wpb_ref, wpc_ref, wout_ref,
                         y_ref, kk_ref, vk_ref, lx_ref, h_ref, cvn_ref,
                         xpad_ref, *, sb, t):
    m = sb * t

    x = x_ref[...].reshape(m, D_MODEL)
    xn = _rms(x, n1_ref[...]).astype(BF16)

    def proj(lo, hi):
        return jnp.dot(xn, win_ref[:, lo:hi], preferred_element_type=F32)

    ri = lax.broadcasted_iota(jnp.int32, (CHUNK, CHUNK), 0)
    ci = lax.broadcasted_iota(jnp.int32, (CHUNK, CHUNK), 1)
    keep = ((ri // t) == (ci // t)) & ((ci % t) <= (ri % t))
    wm = [jnp.where(keep, cws_ref[g], 0.0).astype(BF16) for g in range(N_CHUNK_GROUPS)]
    a_out, v_norm = _chunk_branch(proj(O_UV, O_LX), wm, bs_ref[...], lng_ref[...], lnb_ref[...])
    cvn_ref[...] = v_norm.reshape(sb, t, D_CHUNK)

    lx = proj(O_LX, O_LG)
    xpad_ref[:, 0:8, :] = cs_ref[...]
    xpad_ref[:, 8:8 + t, :] = lx.reshape(sb, t, D_LRU)
    taps = [lx] + [xpad_ref[:, 8 - d:8 - d + t, :].reshape(m, D_LRU) for d in range(1, CONV_W)]
    xc = _conv(taps, cw_ref[...], cb_ref[...])
    a, b = _lru_inputs(xc, lam_ref[...], wai_ref[...], ba_ref[...], bi_ref[...])
    a, b = _seg_scan(a, b, t)
    h0 = jnp.broadcast_to(h0_ref[...], (sb, t, D_LRU)).reshape(m, D_LRU)
    hs = a * h0 + b
    b_out = _gelu(proj(O_LG, O_Q)) * hs
    lx_ref[...] = lx.reshape(sb, t, D_LRU)
    h_ref[...] = hs.reshape(sb, t, D_LRU)

    qn = _head_norm(proj(O_Q, O_KK), bdq_ref[...], gq_ref[...]) * (HEAD_DIM ** -0.5)
    kkn = _head_norm(proj(O_KK, O_VV), bdk_ref[...], gk_ref[...])
    vv = proj(O_VV, O_G)
    lane = lax.broadcasted_iota(jnp.int32, (1, 1, LANES), 2)
    lane_lo = lane < HEAD_DIM
    lane_lo2 = lane_lo.reshape(1, LANES)
    knew = jnp.where(lane_lo2, kkn[:, :LANES], kkn[:, LANES:]).reshape(sb, t, LANES)
    vnew = jnp.where(lane_lo2, vv[:, :LANES], vv[:, LANES:]).reshape(sb, t, LANES)
    kc = jnp.concatenate([ck_ref[...], knew], axis=1)
    vc = jnp.concatenate([cv_ref[...], vnew], axis=1)
    kk_ref[...] = kc[:, t:, :]
    vk_ref[...] = vc[:, t:, :]
    q3 = qn.reshape(sb, t, D_ATTN)
    q3r = pltpu.roll(qn, HEAD_DIM, 1).reshape(sb, t, D_ATTN)
    pieces = []
    for hd in range(N_HEADS):
        kv = hd // (N_HEADS // N_KV_HEADS)
        if hd % 2 == kv:
            src, col = q3, hd // 2
        else:
            src, col = q3r, (hd + 1) // 2
        msk = lane_lo if kv == 0 else jnp.logical_not(lane_lo)
        pieces.append(jnp.where(msk, src[:, :, col * LANES:(col + 1) * LANES], 0.0))
    qs = jnp.concatenate(pieces, axis=1).astype(BF16)
    lg = jnp.einsum('bqd,bkd->bqk', qs, kc.astype(BF16), preferred_element_type=F32)
    lg = lg + bias_ref[...]
    sink = sink_ref[:, 0:1]
    mx = jnp.maximum(jnp.max(lg, axis=-1, keepdims=True), sink)
    p = jnp.exp(lg - mx)
    den = jnp.sum(p, axis=-1, keepdims=True) + jnp.exp(sink - mx)
    o = jnp.einsum('bqk,bkd->bqd', p.astype(BF16), vc.astype(BF16), preferred_element_type=F32) / den
    o_r = pltpu.roll(o.reshape(sb * N_HEADS * t, LANES), HEAD_DIM, 1).reshape(sb, N_HEADS * t, LANES)
    cols = []
    for j in range(N_HEADS // 2):
        kv = j // 2
        lo_src = o if kv == 0 else o_r
        hi_src = o_r if kv == 0 else o
        cols.append(jnp.where(lane_lo, lo_src[:, 2 * j * t:(2 * j + 1) * t, :],
                              hi_src[:, (2 * j + 1) * t:(2 * j + 2) * t, :]))
    c_out = jnp.concatenate(cols, axis=-1).reshape(m, D_ATTN)

    y = _merge(x, a_out, b_out, c_out, proj(O_G, D_INP),
               wpa_ref[...], wpb_ref[...], wpc_ref[...], wout_ref[...])
    y_ref[...] = y.reshape(sb, t, D_MODEL)


def _mixer_sample(x, ck, cv, layer, cs, h0, lw, sb=32):
    nb, t, _ = x.shape
    assert nb % sb == 0 and t == 8 and (sb * t) % CHUNK == 0
    blk = lambda r, w: pl.BlockSpec((sb, r, w), lambda i: (i, 0, 0))
    cache_blk = pl.BlockSpec((sb, WINDOW, LANES), lambda i: (layer * (nb // sb) + i, 0, 0))
    consts = [lw['n1'], lw['win'], lw['cws_s'], lw['bs_s'], lw['lng'], lw['lnb'], lw['cw'], lw['cb'],
              lw['lam'], lw['wai'], lw['ba'], lw['bi'], lw['bdq'], lw['bdk'], lw['gq'], lw['gk'],
              lw['bias_s'], lw['sink_s'], lw['wpa'], lw['wpb'], lw['wpc'], lw['wout']]
    out_shape = [jax.ShapeDtypeStruct((nb, t, D_MODEL), F32),
                 jax.ShapeDtypeStruct((nb, WINDOW, LANES), F32),
                 jax.ShapeDtypeStruct((nb, WINDOW, LANES), F32),
                 jax.ShapeDtypeStruct((nb, t, D_LRU), F32),
                 jax.ShapeDtypeStruct((nb, t, D_LRU), F32),
                 jax.ShapeDtypeStruct((nb, t, D_CHUNK), F32)]
    return pl.pallas_call(
        functools.partial(_mixer_sample_kernel, sb=sb, t=t),
        grid=(nb // sb,),
        in_specs=[blk(t, D_MODEL), cache_blk, cache_blk, blk(8, D_LRU), blk(1, D_LRU)]
                 + [_resident(a.shape) for a in consts],
        out_specs=[blk(t, D_MODEL), blk(WINDOW, LANES), blk(WINDOW, LANES),
                   blk(t, D_LRU), blk(t, D_LRU), blk(t, D_CHUNK)],
        out_shape=out_shape,
        scratch_shapes=[pltpu.VMEM((sb, 8 + t, D_LRU), F32)],
        compiler_params=pltpu.CompilerParams(dimension_semantics=("parallel",),
                                             vmem_limit_bytes=VMEM_LIMIT),
        name="mixer_sample",
    )(x, ck, cv, cs, h0, *consts)


def _ffn_kernel(x_ref, g_ref, wg_ref, wu_ref, wd_ref, o_ref, *, tf):
    x = x_ref[...]
    xn = _rms(x, g_ref[...]).astype(BF16)
    acc = x
    for f in range(D_FF // tf):
        cols = slice(f * tf, (f + 1) * tf)
        gate = jnp.dot(xn, wg_ref[:, cols], preferred_element_type=F32)
        up = jnp.dot(xn, wu_ref[:, cols], preferred_element_type=F32)
        acc = acc + _mm(gate * jax.nn.sigmoid(gate) * up, wd_ref[cols, :])
    o_ref[...] = acc


def _ffn(x, g, wg, wu, wd, tm=512, tf=512):
    n = x.shape[0]
    tm = min(tm, n)
    assert n % tm == 0 and D_FF % tf == 0
    return pl.pallas_call(
        functools.partial(_ffn_kernel, tf=tf),
        grid=(n // tm,),
        in_specs=[pl.BlockSpec((tm, D_MODEL), lambda i: (i, 0)),
                  _resident(g.shape), _resident(wg.shape), _resident(wu.shape), _resident(wd.shape)],
        out_specs=pl.BlockSpec((tm, D_MODEL), lambda i: (i, 0)),
        out_shape=jax.ShapeDtypeStruct((n, D_MODEL), F32),
        compiler_params=pltpu.CompilerParams(dimension_semantics=("parallel",),
                                             vmem_limit_bytes=VMEM_LIMIT),
        name="ffn_dense",
    )(x, g, wg, wu, wd)


def _router_kernel(x_ref, g_ref, wr_ref, tri_ref, xn_ref, comb_ref, rank_ref, combt_ref, rankt_ref, cnt_ref):
    xn = _rms(x_ref[...], g_ref[...])
    xn_ref[...] = xn.astype(BF16)
    wr = wr_ref[...]
    x_hi = xn.astype(BF16)
    x_lo = (xn - x_hi.astype(F32)).astype(BF16)
    w_hi = wr.astype(BF16)
    w_lo = (wr - w_hi.astype(F32)).astype(BF16)
    logits = (jnp.dot(x_hi, w_hi, preferred_element_type=F32) + jnp.dot(x_lo, w_hi, preferred_element_type=F32)
              + jnp.dot(x_hi, w_lo, preferred_element_type=F32))
    lane = lax.broadcasted_iota(jnp.int32, logits.shape, 1).astype(F32)
    logits = jnp.where(lane < N_EXPERTS, logits, NEG)
    m1 = jnp.max(logits, axis=-1, keepdims=True)
    i1 = jnp.min(jnp.where(logits == m1, lane, float(LANES)), axis=-1, keepdims=True)
    rest = jnp.where(lane == i1, NEG, logits)
    m2 = jnp.max(rest, axis=-1, keepdims=True)
    i2 = jnp.min(jnp.where(rest == m2, lane, float(LANES)), axis=-1, keepdims=True)
    e2 = jnp.exp(m2 - m1)
    w1 = 1.0 / (1.0 + e2)
    comb = jnp.where(lane == i1, w1, 0.0) + jnp.where(lane == i2, e2 * w1, 0.0)
    sel = jnp.where(comb > 0.0, 1.0, 0.0)
    nsub = x_ref.shape[0] // MOE_SUB_TOKENS
    selb = sel.astype(BF16)
    ranks, cnts = [], []
    for h in range(nsub):
        rows = slice(h * MOE_SUB_TOKENS, (h + 1) * MOE_SUB_TOKENS)
        ranks.append(jnp.dot(tri_ref[...], selb[rows], preferred_element_type=F32))
        cnts.append(jnp.sum(sel[rows], axis=0, keepdims=True))
    rank = jnp.concatenate(ranks, axis=0) if nsub > 1 else ranks[0]
    cnts = cnts + [jnp.zeros((8 - nsub, LANES), F32)]
    comb_ref[...] = comb
    rank_ref[...] = rank
    combt_ref[...] = comb.T[:N_EXPERTS]
    rankt_ref[...] = rank.T[:N_EXPERTS]
    cnt_ref[...] = jnp.concatenate(cnts, axis=0)[None]


def _router(x, g, wr, tri, tb):
    n = x.shape[0]
    nblk = n // tb
    row = lambda w: pl.BlockSpec((tb, w), lambda i: (i, 0))
    colb = pl.BlockSpec((N_EXPERTS, tb), lambda i: (0, i))
    return pl.pallas_call(
        _router_kernel,
        grid=(nblk,),
        in_specs=[row(D_MODEL), _resident(g.shape), _resident(wr.shape), _resident(tri.shape)],
        out_specs=[row(D_MODEL), row(LANES), row(LANES), colb, colb,
                   pl.BlockSpec((1, 8, LANES), lambda i: (i, 0, 0))],
        out_shape=[jax.ShapeDtypeStruct((n, D_MODEL), BF16),
                   jax.ShapeDtypeStruct((n, LANES), F32),
                   jax.ShapeDtypeStruct((n, LANES), F32),
                   jax.ShapeDtypeStruct((N_EXPERTS, n), F32),
                   jax.ShapeDtypeStruct((N_EXPERTS, n), F32),
                   jax.ShapeDtypeStruct((nblk, 8, LANES), F32)],
        compiler_params=pltpu.CompilerParams(dimension_semantics=("parallel",),
                                             vmem_limit_bytes=VMEM_LIMIT),
        name="moe_router",
    )(x, g, wr, tri)


def _moe_kernel(meta_ref, xn_ref, comb_ref, rank_ref, combt_ref, rankt_ref, wg_ref, wu_ref, wd_ref,
                yin_ref, yout_ref, *, tb, resident):
    nsub = tb // MOE_SUB_TOKENS
    if resident:
        e = pl.program_id(0)
        cnt0 = 1 + e * nsub
    else:
        e = meta_ref[0]
        cnt0 = 1 + pl.program_id(0) * nsub
    nmax = jnp.int32(0)
    for h in range(nsub):
        nmax = jnp.maximum(nmax, meta_ref[cnt0 + h])

    lane_e = lax.broadcasted_iota(jnp.int32, (tb, LANES), 1) == e
    cw_col = jnp.sum(jnp.where(lane_e, comb_ref[...], 0.0), axis=-1, keepdims=True)
    rk_col = jnp.sum(jnp.where(lane_e, rank_ref[...], 0.0), axis=-1, keepdims=True)
    cw_row = combt_ref[pl.ds(e, 1), :]
    rk_row = rankt_ref[pl.ds(e, 1), :]
    rk_col = jnp.where(cw_col > 0.0, rk_col, -1.0)
    rk_row = jnp.where(cw_row > 0.0, rk_row, -1.0)

    def tile(cap, row0, src_ref):
        j0 = row0.astype(F32)
        sub_i = lax.broadcasted_iota(jnp.int32, (cap, MOE_SUB_TOKENS), 0).astype(F32)
        lane_i = lax.broadcasted_iota(jnp.int32, (MOE_SUB_TOKENS, cap), 1).astype(F32)
        xg = []
        for h in range(nsub):
            tok = slice(h * MOE_SUB_TOKENS, (h + 1) * MOE_SUB_TOKENS)
            gather = jnp.where(rk_row[:, tok] - j0 == sub_i, 1.0, 0.0).astype(BF16)
            xg.append(jnp.dot(gather, xn_ref[tok, :], preferred_element_type=F32))
        xg = (jnp.concatenate(xg, axis=0) if nsub > 1 else xg[0]).astype(BF16)
        gate = jnp.dot(xg, wg_ref[0], preferred_element_type=F32)
        up = jnp.dot(xg, wu_ref[0], preferred_element_type=F32)
        o = _mm(gate * jax.nn.sigmoid(gate) * up, wd_ref[0]).astype(BF16)
        for h in range(nsub):
            tok = slice(h * MOE_SUB_TOKENS, (h + 1) * MOE_SUB_TOKENS)
            scatter = jnp.where(rk_col[tok] - j0 == lane_i, 1.0, 0.0).astype(BF16)
            part = cw_col[tok] * jnp.dot(scatter, o[h * cap:(h + 1) * cap], preferred_element_type=F32)
            yout_ref[tok, :] = src_ref[tok, :] + part

    step = MOE_PASS_ROWS[1] - MOE_PASS_ROWS[0]
    cls = jnp.clip(lax.div(nmax - MOE_PASS_ROWS[0] + (step - 1), jnp.int32(step)), 0, len(MOE_PASS_ROWS) - 1)
    if resident:
        @pl.when(e == 0)
        def _():
            yout_ref[...] = yin_ref[...]
    first_src = yout_ref if resident else yin_ref
    lax.switch(cls, [functools.partial(tile, cap, jnp.int32(0), first_src) for cap in MOE_PASS_ROWS])
    big, small = MOE_PASS_ROWS[-1], MOE_EXTRA_ROWS
    nextra = lax.div(jnp.maximum(nmax - big, 0) + (small - 1), jnp.int32(small))

    def loop_body(j, carry):
        tile(small, big + j * small, yout_ref)
        return carry

    lax.fori_loop(0, nextra, loop_body, 0)


def _moe_expert(meta, xn, comb, rank, combt, rankt, wg, wu, wd, y, tb, resident):
    n = xn.shape[0]
    if resident:
        assert n == tb
        grid = (N_EXPERTS,)
        tok_idx = lambda i, m: 0
        exp_idx = lambda i, m: i
    else:
        grid = (n // tb,)
        tok_idx = lambda i, m: i
        exp_idx = lambda i, m: m[0]
    row = lambda w: pl.BlockSpec((tb, w), lambda i, m: (tok_idx(i, m), 0))
    colb = pl.BlockSpec((N_EXPERTS, tb), lambda i, m: (0, tok_idx(i, m)))
    wspec = lambda r, w: pl.BlockSpec((1, r, w), lambda i, m: (exp_idx(i, m), 0, 0), pipeline_mode=pl.Buffered(1))
    grid_spec = pltpu.PrefetchScalarGridSpec(
        num_scalar_prefetch=1,
        grid=grid,
        in_specs=[row(D_MODEL), row(LANES), row(LANES), colb, colb,
                  wspec(D_MODEL, D_FF), wspec(D_MODEL, D_FF), wspec(D_FF, D_MODEL), row(D_MODEL)],
        out_specs=row(D_MODEL),
    )
    return pl.pallas_call(
        functools.partial(_moe_kernel, tb=tb, resident=resident),
        grid_spec=grid_spec,
        out_shape=jax.ShapeDtypeStruct((n, D_MODEL), F32),
        compiler_params=pltpu.CompilerParams(dimension_semantics=("arbitrary",),
                                             vmem_limit_bytes=VMEM_LIMIT),
        name="moe_resident" if resident else "moe_expert",
    )(meta, xn, comb, rank, combt, rankt, wg, wu, wd, y)


def _moe_ffn(x, g, wr, tri, wg, wu, wd, tb):
    xn, comb, rank, combt, rankt, cnt = _router(x, g, wr, tri, tb)
    nsub = tb // MOE_SUB_TOKENS
    cnt = jnp.transpose(cnt[:, :nsub, :N_EXPERTS].astype(jnp.int32), (2, 0, 1)).reshape(N_EXPERTS, -1)
    zero = jnp.zeros((1,), jnp.int32)
    if x.shape[0] == tb:
        return _moe_expert(jnp.concatenate([zero, cnt.reshape(-1)]), xn, comb, rank, combt, rankt,
                           wg, wu, wd, x, tb, True)
    y = x
    for e in range(N_EXPERTS):
        y = _moe_expert(jnp.concatenate([zero + e, cnt[e]]), xn, comb, rank, combt, rankt, wg, wu, wd, y, tb, False)
    return y


def _block_diag_mean(n, width):
    i = jnp.arange(n) // width
    return jnp.where(i[:, None] == i[None, :], 1.0 / width, 0.0).astype(BF16)


def _alibi_bias(nq, nk):
    dist = (WINDOW + jnp.arange(nq)[:, None] - jnp.arange(nk)[None, :]).astype(F32)
    allowed = (dist >= 0) & (dist <= WINDOW)
    slopes = jnp.exp2(-8.0 * (jnp.arange(N_HEADS, dtype=F32) + 1.0) / N_HEADS)
    return jnp.where(allowed[None], -slopes[:, None, None] * dist[None], NEG)


def _pack_layer(l, p, t_s):
    w_in = p['w_in'][l].astype(BF16)
    dup = lambda w: jnp.concatenate([w[:, i * HEAD_DIM:(i + 1) * HEAD_DIM] for i in range(N_KV_HEADS) for _ in (0, 1)], axis=1)
    s_q, s_k = 2 * D_CHUNK + 2 * D_LRU, 2 * D_CHUNK + 2 * D_LRU + D_ATTN
    s_v = s_k + N_KV_HEADS * HEAD_DIM
    s_g = s_v + N_KV_HEADS * HEAD_DIM
    win = jnp.concatenate([w_in[:, :s_k], dup(w_in[:, s_k:s_v]), dup(w_in[:, s_v:s_g]), w_in[:, s_g:]], axis=1)
    assert win.shape[1] == D_INP
    nblk = D_LRU // p['lru_wa'].shape[-1]
    bd = lambda w: jax.scipy.linalg.block_diag(*[w[i] for i in range(nblk)])
    gw = D_CHUNK // N_CHUNK_GROUPS
    rep = CHUNK // t_s
    bias_s = _alibi_bias(t_s, WINDOW + t_s).reshape(N_HEADS * t_s, WINDOW + t_s)
    return {
        'n1': p['norm1_g'][l][None], 'win': win,
        'cws_p': p['chunk_ws'][l],
        'cws_s': jnp.tile(p['chunk_ws'][l][:, :t_s, :t_s], (1, rep, rep)),
        'bs_p': jnp.repeat(p['chunk_bs'][l].T, gw, axis=1),
        'bs_s': jnp.tile(jnp.repeat(p['chunk_bs'][l][:, :t_s].T, gw, axis=1), (rep, 1)),
        'lng': p['chunk_ln_g'][l][None], 'lnb': p['chunk_ln_b'][l][None],
        'cw': p['conv_w'][l], 'cb': p['conv_b'][l][None], 'lam': p['lru_lambda'][l][None],
        'wai': jnp.concatenate([bd(p['lru_wa'][l]), bd(p['lru_wi'][l])], axis=1).astype(BF16),
        'ba': p['lru_ba'][l][None], 'bi': p['lru_bi'][l][None],
        'bdq': _block_diag_mean(D_ATTN, HEAD_DIM), 'bdk': _block_diag_mean(2 * LANES, LANES),
        'gq': jnp.tile(p['q_norm_g'][l], N_HEADS)[None],
        'gk': jnp.tile(p['k_norm_g'][l], 2 * N_KV_HEADS)[None],
        'bias_p': _alibi_bias(CHUNK, 2 * WINDOW), 'bias_s': bias_s,
        'sinks': p['attn_sinks'][l],
        'sink_s': jnp.broadcast_to(jnp.repeat(p['attn_sinks'][l], t_s)[:, None], (N_HEADS * t_s, LANES)),
        'wpa': p['w_proj_a'][l].astype(BF16), 'wpb': p['w_proj_b'][l].astype(BF16),
        'wpc': p['w_proj_c'][l].astype(BF16), 'wout': p['w_out'][l].astype(BF16),
    }


def _strict_lower(n):
    i = jnp.arange(n)
    return jnp.where(i[None, :] < i[:, None], 1.0, 0.0).astype(BF16)


def kernel(x_prompt, x_sample, cache_win_k, cache_win_v, state_conv, state_lru_h, norm1_g, w_in, chunk_ln_g, chunk_ln_b, chunk_ws, chunk_bs, conv_w, conv_b, lru_lambda, lru_wa, lru_ba, lru_wi, lru_bi, q_norm_g, k_norm_g, attn_sinks, w_proj_a, w_proj_b, w_proj_c, w_out, norm2_g, ffn_w_gate, ffn_w_up, ffn_w_down, moe_router, moe_w_gate, moe_w_up, moe_w_down):
    p = dict(norm1_g=norm1_g, w_in=w_in, chunk_ln_g=chunk_ln_g, chunk_ln_b=chunk_ln_b, chunk_ws=chunk_ws,
             chunk_bs=chunk_bs, conv_w=conv_w, conv_b=conv_b, lru_lambda=lru_lambda, lru_wa=lru_wa,
             lru_ba=lru_ba, lru_wi=lru_wi, lru_bi=lru_bi, q_norm_g=q_norm_g, k_norm_g=k_norm_g,
             attn_sinks=attn_sinks, w_proj_a=w_proj_a, w_proj_b=w_proj_b, w_proj_c=w_proj_c, w_out=w_out)
    depth = w_in.shape[0]
    bp, tp, _ = x_prompt.shape
    bs, ts, _ = x_sample.shape
    tb_p = min(1024, bp * tp)
    tb_s = min(1024, bs * ts)
    assert tb_p % MOE_SUB_TOKENS == 0 and tb_s % MOE_SUB_TOKENS == 0
    tri = _strict_lower(MOE_SUB_TOKENS)

    yp, ys = x_prompt, x_sample
    ck = cache_win_k.reshape(depth * bs, WINDOW, LANES)
    cv = cache_win_v.reshape(depth * bs, WINDOW, LANES)
    outs = {k: [] for k in ('pk', 'pv', 'pconv', 'ph', 'sk', 'sv', 'sconv', 'sh', 'schunk')}
    for l in range(depth):
        lw = _pack_layer(l, p, ts)
        yp, pk, pv, plx, phl = _mixer_prompt(yp, lw)
        cs = jnp.pad(state_conv[l], ((0, 0), (8 - (CONV_W - 1), 0), (0, 0)))
        ys, sk, sv, slx, shl, scv = _mixer_sample(ys, ck, cv, l, cs, state_lru_h[l][:, None, :], lw)
        outs['pk'].append(pk.reshape(bp, WINDOW, N_KV_HEADS, HEAD_DIM))
        outs['pv'].append(pv.reshape(bp, WINDOW, N_KV_HEADS, HEAD_DIM))
        outs['pconv'].append(plx[:, 8 - (CONV_W - 1):, :])
        outs['ph'].append(phl[:, 0, :])
        outs['sk'].append(sk.reshape(bs, WINDOW, N_KV_HEADS, HEAD_DIM))
        outs['sv'].append(sv.reshape(bs, WINDOW, N_KV_HEADS, HEAD_DIM))
        outs['sconv'].append(slx[:, ts - (CONV_W - 1):, :])
        outs['sh'].append(shl[:, ts - 1, :])
        outs['schunk'].append(scv)

        g2 = norm2_g[l][None]
        flat_p, flat_s = yp.reshape(bp * tp, D_MODEL), ys.reshape(bs * ts, D_MODEL)
        j = l // 2
        if l % 2 == 0:
            wg, wu, wd = ffn_w_gate[j].astype(BF16), ffn_w_up[j].astype(BF16), ffn_w_down[j].astype(BF16)
            flat_p = _ffn(flat_p, g2, wg, wu, wd)
            flat_s = _ffn(flat_s, g2, wg, wu, wd)
        else:
            wg, wu, wd = moe_w_gate[j].astype(BF16), moe_w_up[j].astype(BF16), moe_w_down[j].astype(BF16)
            wr = jnp.pad(moe_router[j], ((0, 0), (0, LANES - N_EXPERTS)))
            flat_p = _moe_ffn(flat_p, g2, wr, tri, wg, wu, wd, tb_p)
            flat_s = _moe_ffn(flat_s, g2, wr, tri, wg, wu, wd, tb_s)
        yp, ys = flat_p.reshape(bp, tp, D_MODEL), flat_s.reshape(bs, ts, D_MODEL)

    st = {k: jnp.stack(v) for k, v in outs.items()}
    return (yp, ys, st['pk'], st['pv'], st['pconv'], st['ph'],
            st['sk'], st['sv'], st['sconv'], st['sh'], st['schunk'])
```

```python
import functools
import math

import jax
import jax.numpy as jnp
from jax import lax
from jax.experimental import pallas as pl
from jax.experimental.pallas import tpu as pltpu

F32 = jnp.float32
BF16 = jnp.bfloat16

D_MODEL = 1024
D_CHUNK = 512
CHUNK = 128
N_CHUNK_GROUPS = 4
D_LRU = 512
CONV_W = 4
LRU_C = 8.0
N_HEADS = 8
N_KV_HEADS = 2
HEAD_DIM = 64
D_ATTN = N_HEADS * HEAD_DIM
WINDOW = 128
D_FF = 3584
N_EXPERTS = 8
EPS = 1e-6
NEG = -1e30

O_UV = 0
O_LX = 2 * D_CHUNK
O_LG = O_LX + D_LRU
O_Q = O_LG + D_LRU
O_K = O_Q + D_ATTN
O_V = O_K + N_KV_HEADS * HEAD_DIM
O_G = O_V + N_KV_HEADS * HEAD_DIM
D_INP = O_G + 3 * D_MODEL
W_KK = 2 * N_KV_HEADS * HEAD_DIM

LANES = 128
MOE_SUB_TOKENS = 512
MOE_PASS_ROWS = (96, 128, 160, 192)
MOE_EXTRA_ROWS = 48
VMEM_LIMIT = 56 * 1024 * 1024


def _resident(shape):
    nd = len(shape)
    return pl.BlockSpec(shape, lambda *_: (0,) * nd, pipeline_mode=pl.Buffered(1))


def _mm(a, w):
    return jnp.dot(a.astype(BF16), w, preferred_element_type=F32)


def _gelu(x):
    return 0.5 * x * (1.0 + jnp.tanh(0.7978845608028654 * (x + 0.044715 * (x * x * x))))


def _rms(x, g):
    return x * lax.rsqrt(jnp.mean(x * x, axis=-1, keepdims=True) + EPS) * g


def _seg_scan(a, b, seg):
    pos = lax.broadcasted_iota(jnp.int32, (a.shape[0], 1), 0) & (seg - 1)
    s = 1
    while s < seg:
        ok = pos >= s
        a_sh = pltpu.roll(a, s, 0)
        b_sh = pltpu.roll(b, s, 0)
        b = jnp.where(ok, a * b_sh + b, b)
        a = jnp.where(ok, a * a_sh, a)
        s *= 2
    return a, b


def _chunk_branch(uv, wm, bs_full, ln_g, ln_b):
    uv = _gelu(uv)
    u = uv[:, :D_CHUNK]
    v = uv[:, D_CHUNK:]
    mu = jnp.mean(v, axis=-1, keepdims=True)
    vc = v - mu
    v = vc * lax.rsqrt(jnp.mean(vc * vc, axis=-1, keepdims=True) + EPS) * ln_g + ln_b
    vb = v.astype(BF16)
    nblk = uv.shape[0] // CHUNK
    gw = D_CHUNK // N_CHUNK_GROUPS
    rows = []
    for r in range(nblk):
        cols = []
        for g in range(N_CHUNK_GROUPS):
            vg = vb[r * CHUNK:(r + 1) * CHUNK, g * gw:(g + 1) * gw]
            cols.append(jnp.dot(wm[g], vg, preferred_element_type=F32))
        rows.append(jnp.concatenate(cols, axis=-1) + bs_full)
    s = jnp.concatenate(rows, axis=0) if nblk > 1 else rows[0]
    return u * s, v


def _lru_inputs(xc, lam, w_ai, b_a, b_i):
    ri = _mm(xc, w_ai)
    r = jax.nn.sigmoid(ri[:, :D_LRU] + b_a)
    i = jax.nn.sigmoid(ri[:, D_LRU:] + b_i)
    log_sig = jnp.minimum(lam, 0.0) - jnp.log1p(jnp.exp(-jnp.abs(lam)))
    log_a = LRU_C * r * log_sig
    a = jnp.exp(log_a)
    t = jnp.tanh(log_a)
    one_m_a2 = -2.0 * t / (1.0 - t)
    return a, jnp.sqrt(one_m_a2) * (i * xc)


def _conv(taps, conv_w, conv_b):
    out = conv_b + conv_w[3:4] * taps[0]
    for d in range(1, CONV_W):
        out = out + conv_w[3 - d:4 - d] * taps[d]
    return out


def _head_norm(x, bd, g):
    ms = jnp.dot((x * x).astype(BF16), bd, preferred_element_type=F32)
    return x * lax.rsqrt(ms + EPS) * g


def _merge(x, a_out, b_out, c_out, gl, wpa, wpb, wpc, wout):
    gates = jax.nn.sigmoid(gl)
    merged = (gates[:, :D_MODEL] * _mm(a_out, wpa)
              + gates[:, D_MODEL:2 * D_MODEL] * _mm(b_out, wpb)
              + gates[:, 2 * D_MODEL:] * _mm(c_out, wpc))
    return x + _mm(merged, wout)


def _mixer_prompt_kernel(sinks_ref, x_ref, n1_ref, win_ref, wkv_ref, cws_ref, bs_ref, lng_ref, lnb_ref,
                         cw_ref, cb_ref, lam_ref, wai_ref, ba_ref, bi_ref, bdq_ref, bdk_ref,
                         gq_ref, gk_ref, bias_ref, wpa_ref, wpb_ref, wpc_ref, wout_ref,
                         y_ref, kk_ref, vk_ref, lx_ref, h_ref,
                         xpad_ref, hc_ref, pk_ref, pv_ref, *, bb):
    c = pl.program_id(1)
    m = bb * CHUNK

    @pl.when(c == 0)
    def _():
        xpad_ref[:, 0:8, :] = jnp.zeros((bb, 8, D_LRU), F32)
        hc_ref[...] = jnp.zeros_like(hc_ref)
        pk_ref[...] = jnp.zeros_like(pk_ref)
        pv_ref[...] = jnp.zeros_like(pv_ref)

    x = x_ref[...].reshape(m, D_MODEL)
    xn = _rms(x, n1_ref[...]).astype(BF16)

    def proj(lo, hi):
        return jnp.dot(xn, win_ref[:, lo:hi], preferred_element_type=F32)

    ri = lax.broadcasted_iota(jnp.int32, (CHUNK, CHUNK), 0)
    ci = lax.broadcasted_iota(jnp.int32, (CHUNK, CHUNK), 1)
    wm = [jnp.where(ri >= ci, cws_ref[g], 0.0).astype(BF16) for g in range(N_CHUNK_GROUPS)]
    a_out, _ = _chunk_branch(proj(O_UV, O_LX), wm, bs_ref[...], lng_ref[...], lnb_ref[...])

    lx = proj(O_LX, O_LG)
    xpad_ref[:, 8:8 + CHUNK, :] = lx.reshape(bb, CHUNK, D_LRU)
    taps = [lx] + [xpad_ref[:, 8 - d:8 - d + CHUNK, :].reshape(m, D_LRU) for d in range(1, CONV_W)]
    xc = _conv(taps, cw_ref[...], cb_ref[...])
    a, b = _lru_inputs(xc, lam_ref[...], wai_ref[...], ba_ref[...], bi_ref[...])
    a, b = _seg_scan(a, b, CHUNK)
    hs = []
    for r in range(bb):
        rows = slice(r * CHUNK, (r + 1) * CHUNK)
        h_r = a[rows] * hc_ref[r] + b[rows]
        hc_ref[r] = h_r[CHUNK - 1:CHUNK]
        hs.append(h_r)
    hs = jnp.concatenate(hs, axis=0) if bb > 1 else hs[0]
    b_out = _gelu(proj(O_LG, O_Q)) * hs
    tail = xpad_ref[:, CHUNK:CHUNK + 8, :]
    xpad_ref[:, 0:8, :] = tail
    lx_ref[...] = tail
    h_ref[...] = hc_ref[...]

    qn = _head_norm(proj(O_Q, O_K), bdq_ref[...], gq_ref[...]) * (HEAD_DIM ** -0.5)
    kkn = _head_norm(jnp.dot(xn, wkv_ref[:, :W_KK], preferred_element_type=F32), bdk_ref[...], gk_ref[...])
    vv = jnp.dot(xn, wkv_ref[:, W_KK:], preferred_element_type=F32)
    lane = lax.broadcasted_iota(jnp.int32, (1, LANES), 1)
    lane_lo = lane < HEAD_DIM
    kcol = lax.broadcasted_iota(jnp.int32, (1, 2 * WINDOW), 1)
    pen = jnp.where(kcol < WINDOW, jnp.where(c == 0, NEG, 0.0), 0.0)
    c_rows = []
    for r in range(bb):
        rows = slice(r * CHUNK, (r + 1) * CHUNK)
        kcat = jnp.concatenate([pk_ref[r], kkn[rows]], axis=0).astype(BF16)
        vcat = jnp.concatenate([pv_ref[r], vv[rows]], axis=0).astype(BF16)
        cols = []
        for j in range(N_HEADS // 2):
            kv = j // 2
            qj = qn[rows, j * LANES:(j + 1) * LANES]
            kh = kcat[:, kv * LANES:(kv + 1) * LANES]
            vh = vcat[:, kv * LANES:(kv + 1) * LANES]
            outs = []
            for half in range(2):
                hd = 2 * j + half
                qm = jnp.where(lane_lo if half == 0 else jnp.logical_not(lane_lo), qj, 0.0).astype(BF16)
                lg = lax.dot_general(qm, kh, (((1,), (1,)), ((), ())), preferred_element_type=F32)
                lg = lg + bias_ref[hd] + pen
                sink = sinks_ref[hd]
                mx = jnp.maximum(jnp.max(lg, axis=-1, keepdims=True), sink)
                p = jnp.exp(lg - mx)
                den = jnp.sum(p, axis=-1, keepdims=True) + jnp.exp(sink - mx)
                outs.append(jnp.dot(p.astype(BF16), vh, preferred_element_type=F32) / den)
            cols.append(jnp.where(lane_lo, outs[0], outs[1]))
        c_rows.append(jnp.concatenate(cols, axis=-1))
        pk_ref[r] = kkn[rows]
        pv_ref[r] = vv[rows]
    c_out = jnp.concatenate(c_rows, axis=0) if bb > 1 else c_rows[0]
    kk_ref[...] = jnp.where(lane_lo, kkn[:, :LANES], kkn[:, LANES:]).reshape(bb, CHUNK, LANES)
    vk_ref[...] = jnp.where(lane_lo, vv[:, :LANES], vv[:, LANES:]).reshape(bb, CHUNK, LANES)

    y = _merge(x, a_out, b_out, c_out, proj(O_G, D_INP),
               wpa_ref[...], wpb_ref[...], wpc_ref[...], wout_ref[...])
    y_ref[...] = y.reshape(bb, CHUNK, D_MODEL)


def _mixer_prompt(x, lw, bb=4):
    bsz, t, _ = x.shape
    nc = t // CHUNK
    assert t % CHUNK == 0 and bsz % bb == 0
    seq_blk = lambda w: pl.BlockSpec((bb, CHUNK, w), lambda i, c: (i, c, 0))
    keep_blk = lambda r, w: pl.BlockSpec((bb, r, w), lambda i, c: (i, 0, 0))
    consts = [lw['n1'], lw['win'], lw['wkv'], lw['cws_p'], lw['bs_p'], lw['lng'], lw['lnb'], lw['cw'], lw['cb'],
              lw['lam'], lw['wai'], lw['ba'], lw['bi'], lw['bdq'], lw['bdk'], lw['gq'], lw['gk'],
              lw['bias_p'], lw['wpa'], lw['wpb'], lw['wpc'], lw['wout']]
    out_shape = [jax.ShapeDtypeStruct((bsz, t, D_MODEL), F32),
                 jax.ShapeDtypeStruct((bsz, WINDOW, LANES), F32),
                 jax.ShapeDtypeStruct((bsz, WINDOW, LANES), F32),
                 jax.ShapeDtypeStruct((bsz, 8, D_LRU), F32),
                 jax.ShapeDtypeStruct((bsz, 1, D_LRU), F32)]
    return pl.pallas_call(
        functools.partial(_mixer_prompt_kernel, bb=bb),
        grid=(bsz // bb, nc),
        in_specs=[pl.BlockSpec(memory_space=pltpu.SMEM), seq_blk(D_MODEL)] + [_resident(a.shape) for a in consts],
        out_specs=[seq_blk(D_MODEL), keep_blk(WINDOW, LANES), keep_blk(WINDOW, LANES),
                   keep_blk(8, D_LRU), keep_blk(1, D_LRU)],
        out_shape=out_shape,
        scratch_shapes=[pltpu.VMEM((bb, CHUNK + 8, D_LRU), F32),
                        pltpu.VMEM((bb, 1, D_LRU), F32),
                        pltpu.VMEM((bb, CHUNK, 2 * LANES), F32),
                        pltpu.VMEM((bb, CHUNK, 2 * LANES), F32)],
        compiler_params=pltpu.CompilerParams(dimension_semantics=("parallel", "arbitrary"),
                                             vmem_limit_bytes=VMEM_LIMIT),
        name="mixer_prompt",
    )(lw['sinks'], x, *consts)


def _mixer_sample_kernel(x_ref, ck_ref, cv_ref, cs_ref, h0_ref, n1_ref, win_ref, wkv_ref, cws_ref, bs_ref,
                         lng_ref, lnb_ref, cw_ref, cb_ref, lam_ref, wai_ref, ba_ref, bi_ref,
                         bdq_ref, bdk_ref, gq_ref, gk_ref, bias_ref, sink_ref,
                         wpa_ref, wpb_ref, wpc_ref, wout_ref,
                         y_ref, kk_ref, vk_ref, lx_ref, h_ref, cvn_ref,
                         xpad_ref, *, sb, t):
    m = sb * t

    x = x_ref[...].reshape(m, D_MODEL)
    xn = _rms(x, n1_ref[...]).astype(BF16)

    def proj(lo, hi):
        return jnp.dot(xn, win_ref[:, lo:hi], preferred_element_type=F32)

    ri = lax.broadcasted_iota(jnp.int32, (CHUNK, CHUNK), 0)
    ci = lax.broadcasted_iota(jnp.int32, (CHUNK, CHUNK), 1)
    keep = ((ri // t) == (ci // t)) & ((ci % t) <= (ri % t))
    wm = [jnp.where(keep, cws_ref[g], 0.0).astype(BF16) for g in range(N_CHUNK_GROUPS)]
    a_out, v_norm = _chunk_branch(proj(O_UV, O_LX), wm, bs_ref[...], lng_ref[...], lnb_ref[...])
    cvn_ref[...] = v_norm.reshape(sb, t, D_CHUNK)

    lx = proj(O_LX, O_LG)
    xpad_ref[:, 0:8, :] = cs_ref[...]
    xpad_ref[:, 8:8 + t, :] = lx.reshape(sb, t, D_LRU)
    taps = [lx] + [xpad_ref[:, 8 - d:8 - d + t, :].reshape(m, D_LRU) for d in range(1, CONV_W)]
    xc = _conv(taps, cw_ref[...], cb_ref[...])
    a, b = _lru_inputs(xc, lam_ref[...], wai_ref[...], ba_ref[...], bi_ref[...])
    a, b = _seg_scan(a, b, t)
    h0 = jnp.broadcast_to(h0_ref[...], (sb, t, D_LRU)).reshape(m, D_LRU)
    hs = a * h0 + b
    b_out = _gelu(proj(O_LG, O_Q)) * hs
    lx_ref[...] = lx.reshape(sb, t, D_LRU)
    h_ref[...] = hs.reshape(sb, t, D_LRU)

    qn = _head_norm(proj(O_Q, O_K), bdq_ref[...], gq_ref[...]) * (HEAD_DIM ** -0.5)
    kkn = _head_norm(jnp.dot(xn, wkv_ref[:, :W_KK], preferred_element_type=F32), bdk_ref[...], gk_ref[...])
    vv = jnp.dot(xn, wkv_ref[:, W_KK:], preferred_element_type=F32)
    lane = lax.broadcasted_iota(jnp.int32, (1, 1, LANES), 2)
    lane_lo = lane < HEAD_DIM
    lane_lo2 = lane_lo.reshape(1, LANES)
    knew = jnp.where(lane_lo2, kkn[:, :LANES], kkn[:, LANES:]).reshape(sb, t, LANES)
    vnew = jnp.where(lane_lo2, vv[:, :LANES], vv[:, LANES:]).reshape(sb, t, LANES)
    kc = jnp.concatenate([ck_ref[...], knew], axis=1)
    vc = jnp.concatenate([cv_ref[...], vnew], axis=1)
    kk_ref[...] = kc[:, t:, :]
    vk_ref[...] = vc[:, t:, :]
    q3 = qn.reshape(sb, t, D_ATTN)
    q3r = pltpu.roll(qn, HEAD_DIM, 1).reshape(sb, t, D_ATTN)
    pieces = []
    for hd in range(N_HEADS):
        kv = hd // (N_HEADS // N_KV_HEADS)
        if hd % 2 == kv:
            src, col = q3, hd // 2
        else:
            src, col = q3r, (hd + 1) // 2
        msk = lane_lo if kv == 0 else jnp.logical_not(lane_lo)
        pieces.append(jnp.where(msk, src[:, :, col * LANES:(col + 1) * LANES], 0.0))
    qs = jnp.concatenate(pieces, axis=1).astype(BF16)
    lg = jnp.einsum('bqd,bkd->bqk', qs, kc.astype(BF16), preferred_element_type=F32)
    lg = lg + bias_ref[...]
    sink = sink_ref[:, 0:1]
    mx = jnp.maximum(jnp.max(lg, axis=-1, keepdims=True), sink)
    p = jnp.exp(lg - mx)
    den = jnp.sum(p, axis=-1, keepdims=True) + jnp.exp(sink - mx)
    o = jnp.einsum('bqk,bkd->bqd', p.astype(BF16), vc.astype(BF16), preferred_element_type=F32) / den
    o_r = pltpu.roll(o.reshape(sb * N_HEADS * t, LANES), HEAD_DIM, 1).reshape(sb, N_HEADS * t, LANES)
    cols = []
    for j in range(N_HEADS // 2):
        kv = j // 2
        lo_src = o if kv == 0 else o_r
        hi_src = o_r if kv == 0 else o
        cols.append(jnp.where(lane_lo, lo_src[:, 2 * j * t:(2 * j + 1) * t, :],
                              hi_src[:, (2 * j + 1) * t:(2 * j + 2) * t, :]))
    c_out = jnp.concatenate(cols, axis=-1).reshape(m, D_ATTN)

    y = _merge(x, a_out, b_out, c_out, proj(O_G, D_INP),
               wpa_ref[...], wpb_ref[...], wpc_ref[...], wout_ref[...])
    y_ref[...] = y.reshape(sb, t, D_MODEL)


def _mixer_sample(x, ck, cv, layer, cs, h0, lw, sb=32):
    nb, t, _ = x.shape
    assert nb % sb == 0 and t == 8 and (sb * t) % CHUNK == 0
    blk = lambda r, w: pl.BlockSpec((sb, r, w), lambda i: (i, 0, 0))
    cache_blk = pl.BlockSpec((sb, WINDOW, LANES), lambda i: (layer * (nb // sb) + i, 0, 0))
    consts = [lw['n1'], lw['win'], lw['wkv'], lw['cws_s'], lw['bs_s'], lw['lng'], lw['lnb'], lw['cw'], lw['cb'],
              lw['lam'], lw['wai'], lw['ba'], lw['bi'], lw['bdq'], lw['bdk'], lw['gq'], lw['gk'],
              lw['bias_s'], lw['sink_s'], lw['wpa'], lw['wpb'], lw['wpc'], lw['wout']]
    out_shape = [jax.ShapeDtypeStruct((nb, t, D_MODEL), F32),
                 jax.ShapeDtypeStruct((nb, WINDOW, LANES), F32),
                 jax.ShapeDtypeStruct((nb, WINDOW, LANES), F32),
                 jax.ShapeDtypeStruct((nb, t, D_LRU), F32),
                 jax.ShapeDtypeStruct((nb, t, D_LRU), F32),
                 jax.ShapeDtypeStruct((nb, t, D_CHUNK), F32)]
    return pl.pallas_call(
        functools.partial(_mixer_sample_kernel, sb=sb, t=t),
        grid=(nb // sb,),
        in_specs=[blk(t, D_MODEL), cache_blk, cache_blk, blk(8, D_LRU), blk(1, D_LRU)]
                 + [_resident(a.shape) for a in consts],
        out_specs=[blk(t, D_MODEL), blk(WINDOW, LANES), blk(WINDOW, LANES),
                   blk(t, D_LRU), blk(t, D_LRU), blk(t, D_CHUNK)],
        out_shape=out_shape,
        scratch_shapes=[pltpu.VMEM((sb, 8 + t, D_LRU), F32)],
        compiler_params=pltpu.CompilerParams(dimension_semantics=("parallel",),
                                             vmem_limit_bytes=VMEM_LIMIT),
        name="mixer_sample",
    )(x, ck, cv, cs, h0, *consts)


def _ffn_kernel(x_ref, g_ref, wg_ref, wu_ref, wd_ref, o_ref, *, tf):
    x = x_ref[...]
    xn = _rms(x, g_ref[...]).astype(BF16)
    acc = x
    for f in range(D_FF // tf):
        cols = slice(f * tf, (f + 1) * tf)
        gate = jnp.dot(xn, wg_ref[:, cols], preferred_element_type=F32)
        up = jnp.dot(xn, wu_ref[:, cols], preferred_element_type=F32)
        acc = acc + _mm(gate * jax.nn.sigmoid(gate) * up, wd_ref[cols, :])
    o_ref[...] = acc


def _ffn(x, g, wg, wu, wd, tm=1024, tf=512):
    n = x.shape[0]
    tm = min(tm, n)
    assert n % tm == 0 and D_FF % tf == 0
    return pl.pallas_call(
        functools.partial(_ffn_kernel, tf=tf),
        grid=(n // tm,),
        in_specs=[pl.BlockSpec((tm, D_MODEL), lambda i: (i, 0)),
                  _resident(g.shape), _resident(wg.shape), _resident(wu.shape), _resident(wd.shape)],
        out_specs=pl.BlockSpec((tm, D_MODEL), lambda i: (i, 0)),
        out_shape=jax.ShapeDtypeStruct((n, D_MODEL), F32),
        compiler_params=pltpu.CompilerParams(dimension_semantics=("parallel",),
                                             vmem_limit_bytes=VMEM_LIMIT),
        name="ffn_dense",
    )(x, g, wg, wu, wd)


def _router_kernel(x_ref, g_ref, wr_ref, tri_ref, xn_ref, comb_ref, rank_ref, combt_ref, rankt_ref, cnt_ref):
    xn = _rms(x_ref[...], g_ref[...])
    xn_ref[...] = xn.astype(BF16)
    wr = wr_ref[...]
    x_hi = xn.astype(BF16)
    x_lo = (xn - x_hi.astype(F32)).astype(BF16)
    w_hi = wr.astype(BF16)
    w_lo = (wr - w_hi.astype(F32)).astype(BF16)
    logits = (jnp.dot(x_hi, w_hi, preferred_element_type=F32) + jnp.dot(x_lo, w_hi, preferred_element_type=F32)
              + jnp.dot(x_hi, w_lo, preferred_element_type=F32))
    lane = lax.broadcasted_iota(jnp.int32, logits.shape, 1).astype(F32)
    logits = jnp.where(lane < N_EXPERTS, logits, NEG)
    m1 = jnp.max(logits, axis=-1, keepdims=True)
    i1 = jnp.min(jnp.where(logits == m1, lane, float(LANES)), axis=-1, keepdims=True)
    rest = jnp.where(lane == i1, NEG, logits)
    m2 = jnp.max(rest, axis=-1, keepdims=True)
    i2 = jnp.min(jnp.where(rest == m2, lane, float(LANES)), axis=-1, keepdims=True)
    e2 = jnp.exp(m2 - m1)
    w1 = 1.0 / (1.0 + e2)
    comb = jnp.where(lane == i1, w1, 0.0) + jnp.where(lane == i2, e2 * w1, 0.0)
    sel = jnp.where(comb > 0.0, 1.0, 0.0)
    nsub = x_ref.shape[0] // MOE_SUB_TOKENS
    selb = sel.astype(BF16)
    ranks, cnts = [], []
    for h in range(nsub):
        rows = slice(h * MOE_SUB_TOKENS, (h + 1) * MOE_SUB_TOKENS)
        ranks.append(jnp.dot(tri_ref[...], selb[rows], preferred_element_type=F32))
        cnts.append(jnp.sum(sel[rows], axis=0, keepdims=True))
    rank = jnp.concatenate(ranks, axis=0) if nsub > 1 else ranks[0]
    cnts = cnts + [jnp.zeros((8 - nsub, LANES), F32)]
    comb_ref[...] = comb
    rank_ref[...] = rank
    combt_ref[...] = comb.T[:N_EXPERTS]
    rankt_ref[...] = rank.T[:N_EXPERTS]
    cnt_ref[...] = jnp.concatenate(cnts, axis=0)[None]


def _router(x, g, wr, tri, tb):
    n = x.shape[0]
    nblk = n // tb
    row = lambda w: pl.BlockSpec((tb, w), lambda i: (i, 0))
    colb = pl.BlockSpec((N_EXPERTS, tb), lambda i: (0, i))
    return pl.pallas_call(
        _router_kernel,
        grid=(nblk,),
        in_specs=[row(D_MODEL), _resident(g.shape), _resident(wr.shape), _resident(tri.shape)],
        out_specs=[row(D_MODEL), row(LANES), row(LANES), colb, colb,
                   pl.BlockSpec((1, 8, LANES), lambda i: (i, 0, 0))],
        out_shape=[jax.ShapeDtypeStruct((n, D_MODEL), BF16),
                   jax.ShapeDtypeStruct((n, LANES), F32),
                   jax.ShapeDtypeStruct((n, LANES), F32),
                   jax.ShapeDtypeStruct((N_EXPERTS, n), F32),
                   jax.ShapeDtypeStruct((N_EXPERTS, n), F32),
                   jax.ShapeDtypeStruct((nblk, 8, LANES), F32)],
        compiler_params=pltpu.CompilerParams(dimension_semantics=("parallel",),
                                             vmem_limit_bytes=VMEM_LIMIT),
        name="moe_router",
    )(x, g, wr, tri)


def _moe_kernel(meta_ref, xn_ref, comb_ref, rank_ref, combt_ref, rankt_ref, wg_ref, wu_ref, wd_ref,
                yin_ref, yout_ref, *, tb, resident):
    nsub = tb // MOE_SUB_TOKENS
    if resident:
        e = pl.program_id(0)
        cnt0 = 1 + e * nsub
    else:
        e = meta_ref[0]
        cnt0 = 1 + pl.program_id(0) * nsub
    nmax = jnp.int32(0)
    for h in range(nsub):
        nmax = jnp.maximum(nmax, meta_ref[cnt0 + h])

    lane_e = lax.broadcasted_iota(jnp.int32, (tb, LANES), 1) == e
    cw_col = jnp.sum(jnp.where(lane_e, comb_ref[...], 0.0), axis=-1, keepdims=True)
    rk_col = jnp.sum(jnp.where(lane_e, rank_ref[...], 0.0), axis=-1, keepdims=True)
    cw_row = combt_ref[pl.ds(e, 1), :]
    rk_row = rankt_ref[pl.ds(e, 1), :]
    rk_col = jnp.where(cw_col > 0.0, rk_col, -1.0)
    rk_row = jnp.where(cw_row > 0.0, rk_row, -1.0)

    def tile(cap, row0, src_ref):
        j0 = row0.astype(F32)
        sub_i = lax.broadcasted_iota(jnp.int32, (cap, MOE_SUB_TOKENS), 0).astype(F32)
        lane_i = lax.broadcasted_iota(jnp.int32, (MOE_SUB_TOKENS, cap), 1).astype(F32)
        xg = []
        for h in range(nsub):
            tok = slice(h * MOE_SUB_TOKENS, (h + 1) * MOE_SUB_TOKENS)
            gather = jnp.where(rk_row[:, tok] - j0 == sub_i, 1.0, 0.0).astype(BF16)
            xg.append(jnp.dot(gather, xn_ref[tok, :], preferred_element_type=F32))
        xg = (jnp.concatenate(xg, axis=0) if nsub > 1 else xg[0]).astype(BF16)
        gate = jnp.dot(xg, wg_ref[0], preferred_element_type=F32)
        up = jnp.dot(xg, wu_ref[0], preferred_element_type=F32)
        o = _mm(gate * jax.nn.sigmoid(gate) * up, wd_ref[0]).astype(BF16)
        for h in range(nsub):
            tok = slice(h * MOE_SUB_TOKENS, (h + 1) * MOE_SUB_TOKENS)
            scatter = jnp.where(rk_col[tok] - j0 == lane_i, 1.0, 0.0).astype(BF16)
            part = cw_col[tok] * jnp.dot(scatter, o[h * cap:(h + 1) * cap], preferred_element_type=F32)
            yout_ref[tok, :] = src_ref[tok, :] + part

    step = MOE_PASS_ROWS[1] - MOE_PASS_ROWS[0]
    cls = jnp.clip(lax.div(nmax - MOE_PASS_ROWS[0] + (step - 1), jnp.int32(step)), 0, len(MOE_PASS_ROWS) - 1)
    if resident:
        @pl.when(e == 0)
        def _():
            yout_ref[...] = yin_ref[...]
    first_src = yout_ref if resident else yin_ref
    lax.switch(cls, [functools.partial(tile, cap, jnp.int32(0), first_src) for cap in MOE_PASS_ROWS])
    big, small = MOE_PASS_ROWS[-1], MOE_EXTRA_ROWS
    nextra = lax.div(jnp.maximum(nmax - big, 0) + (small - 1), jnp.int32(small))

    def loop_body(j, carry):
        tile(small, big + j * small, yout_ref)
        return carry

    lax.fori_loop(0, nextra, loop_body, 0)


def _moe_expert(meta, xn, comb, rank, combt, rankt, wg, wu, wd, y, tb, resident):
    n = xn.shape[0]
    if resident:
        assert n == tb
        grid = (N_EXPERTS,)
        tok_idx = lambda i, m: 0
        exp_idx = lambda i, m: i
    else:
        grid = (n // tb,)
        tok_idx = lambda i, m: i
        exp_idx = lambda i, m: m[0]
    row = lambda w: pl.BlockSpec((tb, w), lambda i, m: (tok_idx(i, m), 0))
    colb = pl.BlockSpec((N_EXPERTS, tb), lambda i, m: (0, tok_idx(i, m)))
    wspec = lambda r, w: pl.BlockSpec((1, r, w), lambda i, m: (exp_idx(i, m), 0, 0), pipeline_mode=pl.Buffered(1))
    grid_spec = pltpu.PrefetchScalarGridSpec(
        num_scalar_prefetch=1,
        grid=grid,
        in_specs=[row(D_MODEL), row(LANES), row(LANES), colb, colb,
                  wspec(D_MODEL, D_FF), wspec(D_MODEL, D_FF), wspec(D_FF, D_MODEL), row(D_MODEL)],
        out_specs=row(D_MODEL),
    )
    return pl.pallas_call(
        functools.partial(_moe_kernel, tb=tb, resident=resident),
        grid_spec=grid_spec,
        out_shape=jax.ShapeDtypeStruct((n, D_MODEL), F32),
        compiler_params=pltpu.CompilerParams(dimension_semantics=("arbitrary",),
                                             vmem_limit_bytes=VMEM_LIMIT),
        name="moe_resident" if resident else "moe_expert",
    )(meta, xn, comb, rank, combt, rankt, wg, wu, wd, y)


def _moe_ffn(x, g, wr, tri, wg, wu, wd, tb):
    xn, comb, rank, combt, rankt, cnt = _router(x, g, wr, tri, tb)
    nsub = tb // MOE_SUB_TOKENS
    cnt = jnp.transpose(cnt[:, :nsub, :N_EXPERTS].astype(jnp.int32), (2, 0, 1)).reshape(N_EXPERTS, -1)
    zero = jnp.zeros((1,), jnp.int32)
    if x.shape[0] == tb:
        return _moe_expert(jnp.concatenate([zero, cnt.reshape(-1)]), xn, comb, rank, combt, rankt,
                           wg, wu, wd, x, tb, True)
    y = x
    for e in range(N_EXPERTS):
        y = _moe_expert(jnp.concatenate([zero + e, cnt[e]]), xn, comb, rank, combt, rankt, wg, wu, wd, y, tb, False)
    return y


def _block_diag_mean(n, width):
    i = jnp.arange(n) // width
    return jnp.where(i[:, None] == i[None, :], 1.0 / width, 0.0).astype(BF16)


def _alibi_bias(nq, nk):
    dist = (WINDOW + jnp.arange(nq)[:, None] - jnp.arange(nk)[None, :]).astype(F32)
    allowed = (dist >= 0) & (dist <= WINDOW)
    slopes = jnp.exp2(-8.0 * (jnp.arange(N_HEADS, dtype=F32) + 1.0) / N_HEADS)
    return jnp.where(allowed[None], -slopes[:, None, None] * dist[None], NEG)


def _pack_layer(l, p, t_s):
    w_in = p['w_in'][l].astype(BF16)
    dup = lambda w: jnp.concatenate([w[:, i * HEAD_DIM:(i + 1) * HEAD_DIM] for i in range(N_KV_HEADS) for _ in (0, 1)], axis=1)
    s_q, s_k = 2 * D_CHUNK + 2 * D_LRU, 2 * D_CHUNK + 2 * D_LRU + D_ATTN
    s_v = s_k + N_KV_HEADS * HEAD_DIM
    s_g = s_v + N_KV_HEADS * HEAD_DIM
    assert (s_k, s_v, s_g, w_in.shape[1]) == (O_K, O_V, O_G, D_INP)
    wkv = jnp.concatenate([dup(w_in[:, s_k:s_v]), dup(w_in[:, s_v:s_g])], axis=1)
    nblk = D_LRU // p['lru_wa'].shape[-1]
    bd = lambda w: jax.scipy.linalg.block_diag(*[w[i] for i in range(nblk)])
    gw = D_CHUNK // N_CHUNK_GROUPS
    rep = CHUNK // t_s
    bias_s = _alibi_bias(t_s, WINDOW + t_s).reshape(N_HEADS * t_s, WINDOW + t_s)
    return {
        'n1': p['norm1_g'][l][None], 'win': w_in, 'wkv': wkv,
        'cws_p': p['chunk_ws'][l],
        'cws_s': jnp.tile(p['chunk_ws'][l][:, :t_s, :t_s], (1, rep, rep)),
        'bs_p': jnp.repeat(p['chunk_bs'][l].T, gw, axis=1),
        'bs_s': jnp.tile(jnp.repeat(p['chunk_bs'][l][:, :t_s].T, gw, axis=1), (rep, 1)),
        'lng': p['chunk_ln_g'][l][None], 'lnb': p['chunk_ln_b'][l][None],
        'cw': p['conv_w'][l], 'cb': p['conv_b'][l][None], 'lam': p['lru_lambda'][l][None],
        'wai': jnp.concatenate([bd(p['lru_wa'][l]), bd(p['lru_wi'][l])], axis=1).astype(BF16),
        'ba': p['lru_ba'][l][None], 'bi': p['lru_bi'][l][None],
        'bdq': _block_diag_mean(D_ATTN, HEAD_DIM), 'bdk': _block_diag_mean(2 * LANES, LANES),
        'gq': jnp.tile(p['q_norm_g'][l], N_HEADS)[None],
        'gk': jnp.tile(p['k_norm_g'][l], 2 * N_KV_HEADS)[None],
        'bias_p': _alibi_bias(CHUNK, 2 * WINDOW), 'bias_s': bias_s,
        'sinks': p['attn_sinks'][l],
        'sink_s': jnp.broadcast_to(jnp.repeat(p['attn_sinks'][l], t_s)[:, None], (N_HEADS * t_s, LANES)),
        'wpa': p['w_proj_a'][l].astype(BF16), 'wpb': p['w_proj_b'][l].astype(BF16),
        'wpc': p['w_proj_c'][l].astype(BF16), 'wout': p['w_out'][l].astype(BF16),
    }


def _strict_lower(n):
    i = jnp.arange(n)
    return jnp.where(i[None, :] < i[:, None], 1.0, 0.0).astype(BF16)


def kernel(x_prompt, x_sample, cache_win_k, cache_win_v, state_conv, state_lru_h, norm1_g, w_in, chunk_ln_g, chunk_ln_b, chunk_ws, chunk_bs, conv_w, conv_b, lru_lambda, lru_wa, lru_ba, lru_wi, lru_bi, q_norm_g, k_norm_g, attn_sinks, w_proj_a, w_proj_b, w_proj_c, w_out, norm2_g, ffn_w_gate, ffn_w_up, ffn_w_down, moe_router, moe_w_gate, moe_w_up, moe_w_down):
    p = dict(norm1_g=norm1_g, w_in=w_in, chunk_ln_g=chunk_ln_g, chunk_ln_b=chunk_ln_b, chunk_ws=chunk_ws,
             chunk_bs=chunk_bs, conv_w=conv_w, conv_b=conv_b, lru_lambda=lru_lambda, lru_wa=lru_wa,
             lru_ba=lru_ba, lru_wi=lru_wi, lru_bi=lru_bi, q_norm_g=q_norm_g, k_norm_g=k_norm_g,
             attn_sinks=attn_sinks, w_proj_a=w_proj_a, w_proj_b=w_proj_b, w_proj_c=w_proj_c, w_out=w_out)
    depth = w_in.shape[0]
    bp, tp, _ = x_prompt.shape
    bs, ts, _ = x_sample.shape
    tb_p = min(1024, bp * tp)
    tb_s = min(1024, bs * ts)
    assert tb_p % MOE_SUB_TOKENS == 0 and tb_s % MOE_SUB_TOKENS == 0
    tri = _strict_lower(MOE_SUB_TOKENS)

    yp, ys = x_prompt, x_sample
    ck = cache_win_k.reshape(depth * bs, WINDOW, LANES)
    cv = cache_win_v.reshape(depth * bs, WINDOW, LANES)
    outs = {k: [] for k in ('pk', 'pv', 'pconv', 'ph', 'sk', 'sv', 'sconv', 'sh', 'schunk')}
    for l in range(depth):
        lw = _pack_layer(l, p, ts)
        yp, pk, pv, plx, phl = _mixer_prompt(yp, lw)
        cs = jnp.pad(state_conv[l], ((0, 0), (8 - (CONV_W - 1), 0), (0, 0)))
        ys, sk, sv, slx, shl, scv = _mixer_sample(ys, ck, cv, l, cs, state_lru_h[l][:, None, :], lw)
        outs['pk'].append(pk.reshape(bp, WINDOW, N_KV_HEADS, HEAD_DIM))
        outs['pv'].append(pv.reshape(bp, WINDOW, N_KV_HEADS, HEAD_DIM))
        outs['pconv'].append(plx[:, 8 - (CONV_W - 1):, :])
        outs['ph'].append(phl[:, 0, :])
        outs['sk'].append(sk.reshape(bs, WINDOW, N_KV_HEADS, HEAD_DIM))
        outs['sv'].append(sv.reshape(bs, WINDOW, N_KV_HEADS, HEAD_DIM))
        outs['sconv'].append(slx[:, ts - (CONV_W - 1):, :])
        outs['sh'].append(shl[:, ts - 1, :])
        outs['schunk'].append(scv)

        g2 = norm2_g[l][None]
        flat_p, flat_s = yp.reshape(bp * tp, D_MODEL), ys.reshape(bs * ts, D_MODEL)
        j = l // 2
        if l % 2 == 0:
            wg, wu, wd = ffn_w_gate[j].astype(BF16), ffn_w_up[j].astype(BF16), ffn_w_down[j].astype(BF16)
            flat_p = _ffn(flat_p, g2, wg, wu, wd)
            flat_s = _ffn(flat_s, g2, wg, wu, wd)
        else:
            wg, wu, wd = moe_w_gate[j].astype(BF16), moe_w_up[j].astype(BF16), moe_w_down[j].astype(BF16)
            wr = jnp.pad(moe_router[j], ((0, 0), (0, LANES - N_EXPERTS)))
            flat_p = _moe_ffn(flat_p, g2, wr, tri, wg, wu, wd, tb_p)
            flat_s = _moe_ffn(flat_s, g2, wr, tri, wg, wu, wd, tb_s)
        yp, ys = flat_p.reshape(bp, tp, D_MODEL), flat_s.reshape(bs, ts, D_MODEL)

    st = {k: jnp.stack(v) for k, v in outs.items()}
    return (yp, ys, st['pk'], st['pv'], st['pconv'], st['ph'],
            st['sk'], st['sv'], st['sconv'], st['sh'], st['schunk'])
```

```python
import functools
import math

import jax
import jax.numpy as jnp
from jax import lax
from jax.experimental import pallas as pl
from jax.experimental.pallas import tpu as pltpu

F32 = jnp.float32
BF16 = jnp.bfloat16

D_MODEL = 1024
D_CHUNK = 512
CHUNK = 128
N_CHUNK_GROUPS = 4
D_LRU = 512
CONV_W = 4
LRU_C = 8.0
N_HEADS = 8
N_KV_HEADS = 2
HEAD_DIM = 64
D_ATTN = N_HEADS * HEAD_DIM
WINDOW = 128
D_FF = 3584
N_EXPERTS = 8
EPS = 1e-6
NEG = -1e30

O_UV = 0
O_LX = 2 * D_CHUNK
O_LG = O_LX + D_LRU
O_Q = O_LG + D_LRU
O_K = O_Q + D_ATTN
O_V = O_K + N_KV_HEADS * HEAD_DIM
O_G = O_V + N_KV_HEADS * HEAD_DIM
D_INP = O_G + 3 * D_MODEL
W_KK = 2 * N_KV_HEADS * HEAD_DIM

LANES = 128
MOE_SUB_TOKENS = 512
MOE_PASS_ROWS = (96, 128, 160, 192)
MOE_EXTRA_ROWS = 48
VMEM_LIMIT = 56 * 1024 * 1024


def _resident(shape):
    nd = len(shape)
    return pl.BlockSpec(shape, lambda *_: (0,) * nd, pipeline_mode=pl.Buffered(1))


def _mm(a, w):
    return jnp.dot(a.astype(BF16), w, preferred_element_type=F32)


def _gelu(x):
    return 0.5 * x * (1.0 + jnp.tanh(0.7978845608028654 * (x + 0.044715 * (x * x * x))))


def _rms(x, g):
    return x * lax.rsqrt(jnp.mean(x * x, axis=-1, keepdims=True) + EPS) * g


def _seg_scan(a, b, seg):
    pos = lax.broadcasted_iota(jnp.int32, (a.shape[0], 1), 0) & (seg - 1)
    s = 1
    while s < seg:
        ok = pos >= s
        a_sh = pltpu.roll(a, s, 0)
        b_sh = pltpu.roll(b, s, 0)
        b = jnp.where(ok, a * b_sh + b, b)
        a = jnp.where(ok, a * a_sh, a)
        s *= 2
    return a, b


def _chunk_branch(uv, wm, bs_full, ln_g, ln_b):
    uv = _gelu(uv)
    u = uv[:, :D_CHUNK]
    v = uv[:, D_CHUNK:]
    mu = jnp.mean(v, axis=-1, keepdims=True)
    vc = v - mu
    v = vc * lax.rsqrt(jnp.mean(vc * vc, axis=-1, keepdims=True) + EPS) * ln_g + ln_b
    vb = v.astype(BF16)
    nblk = uv.shape[0] // CHUNK
    gw = D_CHUNK // N_CHUNK_GROUPS
    rows = []
    for r in range(nblk):
        cols = []
        for g in range(N_CHUNK_GROUPS):
            vg = vb[r * CHUNK:(r + 1) * CHUNK, g * gw:(g + 1) * gw]
            cols.append(jnp.dot(wm[g], vg, preferred_element_type=F32))
        rows.append(jnp.concatenate(cols, axis=-1) + bs_full)
    s = jnp.concatenate(rows, axis=0) if nblk > 1 else rows[0]
    return u * s, v


def _lru_inputs(xc, lam, w_ai, b_a, b_i):
    ri = _mm(xc, w_ai)
    r = jax.nn.sigmoid(ri[:, :D_LRU] + b_a)
    i = jax.nn.sigmoid(ri[:, D_LRU:] + b_i)
    log_sig = jnp.minimum(lam, 0.0) - jnp.log1p(jnp.exp(-jnp.abs(lam)))
    log_a = LRU_C * r * log_sig
    a = jnp.exp(log_a)
    t = jnp.tanh(log_a)
    one_m_a2 = -2.0 * t / (1.0 - t)
    return a, jnp.sqrt(one_m_a2) * (i * xc)


def _conv(taps, conv_w, conv_b):
    out = conv_b + conv_w[3:4] * taps[0]
    for d in range(1, CONV_W):
        out = out + conv_w[3 - d:4 - d] * taps[d]
    return out


def _head_norm(x, bd, g):
    ms = jnp.dot((x * x).astype(BF16), bd, preferred_element_type=F32)
    return x * lax.rsqrt(ms + EPS) * g


def _merge(x, a_out, b_out, c_out, gl, wpa, wpb, wpc, wout):
    gates = jax.nn.sigmoid(gl)
    merged = (gates[:, :D_MODEL] * _mm(a_out, wpa)
              + gates[:, D_MODEL:2 * D_MODEL] * _mm(b_out, wpb)
              + gates[:, 2 * D_MODEL:] * _mm(c_out, wpc))
    return x + _mm(merged, wout)


def _mixer_prompt_kernel(sinks_ref, x_ref, n1_ref, win_ref, wkv_ref, cws_ref, bs_ref, lng_ref, lnb_ref,
                         cw_ref, cb_ref, lam_ref, wai_ref, ba_ref, bi_ref, bdq_ref, bdk_ref,
                         gq_ref, gk_ref, bias_ref, wpa_ref, wpb_ref, wpc_ref, wout_ref,
                         y_ref, kk_ref, vk_ref, lx_ref, h_ref,
                         xpad_ref, hc_ref, pk_ref, pv_ref, *, bb):
    c = pl.program_id(1)
    m = bb * CHUNK

    @pl.when(c == 0)
    def _():
        xpad_ref[:, 0:8, :] = jnp.zeros((bb, 8, D_LRU), F32)
        hc_ref[...] = jnp.zeros_like(hc_ref)
        pk_ref[...] = jnp.zeros_like(pk_ref)
        pv_ref[...] = jnp.zeros_like(pv_ref)

    x = x_ref[...].reshape(m, D_MODEL)
    xn = _rms(x, n1_ref[...]).astype(BF16)

    def proj(lo, hi):
        return jnp.dot(xn, win_ref[:, lo:hi], preferred_element_type=F32)

    ri = lax.broadcasted_iota(jnp.int32, (CHUNK, CHUNK), 0)
    ci = lax.broadcasted_iota(jnp.int32, (CHUNK, CHUNK), 1)
    wm = [jnp.where(ri >= ci, cws_ref[g], 0.0).astype(BF16) for g in range(N_CHUNK_GROUPS)]
    a_out, _ = _chunk_branch(proj(O_UV, O_LX), wm, bs_ref[...], lng_ref[...], lnb_ref[...])

    lx = proj(O_LX, O_LG)
    xpad_ref[:, 8:8 + CHUNK, :] = lx.reshape(bb, CHUNK, D_LRU)
    taps = [lx] + [xpad_ref[:, 8 - d:8 - d + CHUNK, :].reshape(m, D_LRU) for d in range(1, CONV_W)]
    xc = _conv(taps, cw_ref[...], cb_ref[...])
    a, b = _lru_inputs(xc, lam_ref[...], wai_ref[...], ba_ref[...], bi_ref[...])
    a, b = _seg_scan(a, b, 8)
    hs = []
    for r in range(bb):
        carry = hc_ref[r]
        for k in range(CHUNK // 8):
            rows = slice(r * CHUNK + 8 * k, r * CHUNK + 8 * k + 8)
            h_t = a[rows] * carry + b[rows]
            carry = h_t[7:8]
            hs.append(h_t)
        hc_ref[r] = carry
    hs = jnp.concatenate(hs, axis=0)
    b_out = _gelu(proj(O_LG, O_Q)) * hs
    tail = xpad_ref[:, CHUNK:CHUNK + 8, :]
    xpad_ref[:, 0:8, :] = tail
    lx_ref[...] = tail
    h_ref[...] = hc_ref[...]

    qn = _head_norm(proj(O_Q, O_K), bdq_ref[...], gq_ref[...]) * (HEAD_DIM ** -0.5)
    kkn = _head_norm(jnp.dot(xn, wkv_ref[:, :W_KK], preferred_element_type=F32), bdk_ref[...], gk_ref[...])
    vv = jnp.dot(xn, wkv_ref[:, W_KK:], preferred_element_type=F32)
    lane = lax.broadcasted_iota(jnp.int32, (1, LANES), 1)
    lane_lo = lane < HEAD_DIM
    kcol = lax.broadcasted_iota(jnp.int32, (1, 2 * WINDOW), 1)
    pen = jnp.where(kcol < WINDOW, jnp.where(c == 0, NEG, 0.0), 0.0)
    c_rows = []
    for r in range(bb):
        rows = slice(r * CHUNK, (r + 1) * CHUNK)
        kcat = jnp.concatenate([pk_ref[r], kkn[rows]], axis=0).astype(BF16)
        vcat = jnp.concatenate([pv_ref[r], vv[rows]], axis=0).astype(BF16)
        cols = []
        for j in range(N_HEADS // 2):
            kv = j // 2
            qj = qn[rows, j * LANES:(j + 1) * LANES]
            kh = kcat[:, kv * LANES:(kv + 1) * LANES]
            vh = vcat[:, kv * LANES:(kv + 1) * LANES]
            outs = []
            for half in range(2):
                hd = 2 * j + half
                qm = jnp.where(lane_lo if half == 0 else jnp.logical_not(lane_lo), qj, 0.0).astype(BF16)
                lg = lax.dot_general(qm, kh, (((1,), (1,)), ((), ())), preferred_element_type=F32)
                lg = lg + bias_ref[hd] + pen
                sink = sinks_ref[hd]
                mx = jnp.maximum(jnp.max(lg, axis=-1, keepdims=True), sink)
                p = jnp.exp(lg - mx)
                den = jnp.sum(p, axis=-1, keepdims=True) + jnp.exp(sink - mx)
                outs.append(jnp.dot(p.astype(BF16), vh, preferred_element_type=F32) / den)
            cols.append(jnp.where(lane_lo, outs[0], outs[1]))
        c_rows.append(jnp.concatenate(cols, axis=-1))
        pk_ref[r] = kkn[rows]
        pv_ref[r] = vv[rows]
    c_out = jnp.concatenate(c_rows, axis=0) if bb > 1 else c_rows[0]
    kk_ref[...] = jnp.where(lane_lo, kkn[:, :LANES], kkn[:, LANES:]).reshape(bb, CHUNK, LANES)
    vk_ref[...] = jnp.where(lane_lo, vv[:, :LANES], vv[:, LANES:]).reshape(bb, CHUNK, LANES)

    y = _merge(x, a_out, b_out, c_out, proj(O_G, D_INP),
               wpa_ref[...], wpb_ref[...], wpc_ref[...], wout_ref[...])
    y_ref[...] = y.reshape(bb, CHUNK, D_MODEL)


def _mixer_prompt(x, lw, bb=4):
    bsz, t, _ = x.shape
    nc = t // CHUNK
    assert t % CHUNK == 0 and bsz % bb == 0
    seq_blk = lambda w: pl.BlockSpec((bb, CHUNK, w), lambda i, c: (i, c, 0))
    keep_blk = lambda r, w: pl.BlockSpec((bb, r, w), lambda i, c: (i, 0, 0))
    consts = [lw['n1'], lw['win'], lw['wkv'], lw['cws_p'], lw['bs_p'], lw['lng'], lw['lnb'], lw['cw'], lw['cb'],
              lw['lam'], lw['wai'], lw['ba'], lw['bi'], lw['bdq'], lw['bdk'], lw['gq'], lw['gk'],
              lw['bias_p'], lw['wpa'], lw['wpb'], lw['wpc'], lw['wout']]
    out_shape = [jax.ShapeDtypeStruct((bsz, t, D_MODEL), F32),
                 jax.ShapeDtypeStruct((bsz, WINDOW, LANES), F32),
                 jax.ShapeDtypeStruct((bsz, WINDOW, LANES), F32),
                 jax.ShapeDtypeStruct((bsz, 8, D_LRU), F32),
                 jax.ShapeDtypeStruct((bsz, 1, D_LRU), F32)]
    return pl.pallas_call(
        functools.partial(_mixer_prompt_kernel, bb=bb),
        grid=(bsz // bb, nc),
        in_specs=[pl.BlockSpec(memory_space=pltpu.SMEM), seq_blk(D_MODEL)] + [_resident(a.shape) for a in consts],
        out_specs=[seq_blk(D_MODEL), keep_blk(WINDOW, LANES), keep_blk(WINDOW, LANES),
                   keep_blk(8, D_LRU), keep_blk(1, D_LRU)],
        out_shape=out_shape,
        scratch_shapes=[pltpu.VMEM((bb, CHUNK + 8, D_LRU), F32),
                        pltpu.VMEM((bb, 1, D_LRU), F32),
                        pltpu.VMEM((bb, CHUNK, 2 * LANES), F32),
                        pltpu.VMEM((bb, CHUNK, 2 * LANES), F32)],
        compiler_params=pltpu.CompilerParams(dimension_semantics=("parallel", "arbitrary"),
                                             vmem_limit_bytes=VMEM_LIMIT),
        name="mixer_prompt",
    )(lw['sinks'], x, *consts)


def _mixer_sample_kernel(x_ref, ck_ref, cv_ref, cs_ref, h0_ref, n1_ref, win_ref, wkv_ref, cws_ref, bs_ref,
                         lng_ref, lnb_ref, cw_ref, cb_ref, lam_ref, wai_ref, ba_ref, bi_ref,
                         bdq_ref, bdk_ref, gq_ref, gk_ref, bias_ref, sink_ref,
                         wpa_ref, wpb_ref, wpc_ref, wout_ref,
                         y_ref, kk_ref, vk_ref, lx_ref, h_ref, cvn_ref,
                         xpad_ref, *, sb, t):
    m = sb * t

    x = x_ref[...].reshape(m, D_MODEL)
    xn = _rms(x, n1_ref[...]).astype(BF16)

    def proj(lo, hi):
        return jnp.dot(xn, win_ref[:, lo:hi], preferred_element_type=F32)

    ri = lax.broadcasted_iota(jnp.int32, (CHUNK, CHUNK), 0)
    ci = lax.broadcasted_iota(jnp.int32, (CHUNK, CHUNK), 1)
    keep = ((ri // t) == (ci // t)) & ((ci % t) <= (ri % t))
    wm = [jnp.where(keep, cws_ref[g], 0.0).astype(BF16) for g in range(N_CHUNK_GROUPS)]
    a_out, v_norm = _chunk_branch(proj(O_UV, O_LX), wm, bs_ref[...], lng_ref[...], lnb_ref[...])
    cvn_ref[...] = v_norm.reshape(sb, t, D_CHUNK)

    lx = proj(O_LX, O_LG)
    xpad_ref[:, 0:8, :] = cs_ref[...]
    xpad_ref[:, 8:8 + t, :] = lx.reshape(sb, t, D_LRU)
    taps = [lx] + [xpad_ref[:, 8 - d:8 - d + t, :].reshape(m, D_LRU) for d in range(1, CONV_W)]
    xc = _conv(taps, cw_ref[...], cb_ref[...])
    a, b = _lru_inputs(xc, lam_ref[...], wai_ref[...], ba_ref[...], bi_ref[...])
    a, b = _seg_scan(a, b, t)
    h0 = jnp.broadcast_to(h0_ref[...], (sb, t, D_LRU)).reshape(m, D_LRU)
    hs = a * h0 + b
    b_out = _gelu(proj(O_LG, O_Q)) * hs
    lx_ref[...] = lx.reshape(sb, t, D_LRU)
    h_ref[...] = hs.reshape(sb, t, D_LRU)

    qn = _head_norm(proj(O_Q, O_K), bdq_ref[...], gq_ref[...]) * (HEAD_DIM ** -0.5)
    kkn = _head_norm(jnp.dot(xn, wkv_ref[:, :W_KK], preferred_element_type=F32), bdk_ref[...], gk_ref[...])
    vv = jnp.dot(xn, wkv_ref[:, W_KK:], preferred_element_type=F32)
    lane = lax.broadcasted_iota(jnp.int32, (1, 1, LANES), 2)
    lane_lo = lane < HEAD_DIM
    lane_lo2 = lane_lo.reshape(1, LANES)
    knew = jnp.where(lane_lo2, kkn[:, :LANES], kkn[:, LANES:]).reshape(sb, t, LANES)
    vnew = jnp.where(lane_lo2, vv[:, :LANES], vv[:, LANES:]).reshape(sb, t, LANES)
    kc = jnp.concatenate([ck_ref[...], knew], axis=1)
    vc = jnp.concatenate([cv_ref[...], vnew], axis=1)
    kk_ref[...] = kc[:, t:, :]
    vk_ref[...] = vc[:, t:, :]
    q3 = qn.reshape(sb, t, D_ATTN)
    q3r = pltpu.roll(qn, HEAD_DIM, 1).reshape(sb, t, D_ATTN)
    pieces = []
    for hd in range(N_HEADS):
        kv = hd // (N_HEADS // N_KV_HEADS)
        if hd % 2 == kv:
            src, col = q3, hd // 2
        else:
            src, col = q3r, (hd + 1) // 2
        msk = lane_lo if kv == 0 else jnp.logical_not(lane_lo)
        pieces.append(jnp.where(msk, src[:, :, col * LANES:(col + 1) * LANES], 0.0))
    qs = jnp.concatenate(pieces, axis=1).astype(BF16)
    lg = jnp.einsum('bqd,bkd->bqk', qs, kc.astype(BF16), preferred_element_type=F32)
    lg = lg + bias_ref[...]
    sink = sink_ref[:, 0:1]
    mx = jnp.maximum(jnp.max(lg, axis=-1, keepdims=True), sink)
    p = jnp.exp(lg - mx)
    den = jnp.sum(p, axis=-1, keepdims=True) + jnp.exp(sink - mx)
    o = jnp.einsum('bqk,bkd->bqd', p.astype(BF16), vc.astype(BF16), preferred_element_type=F32) / den
    o_r = pltpu.roll(o.reshape(sb * N_HEADS * t, LANES), HEAD_DIM, 1).reshape(sb, N_HEADS * t, LANES)
    cols = []
    for j in range(N_HEADS // 2):
        kv = j // 2
        lo_src = o if kv == 0 else o_r
        hi_src = o_r if kv == 0 else o
        cols.append(jnp.where(lane_lo, lo_src[:, 2 * j * t:(2 * j + 1) * t, :],
                              hi_src[:, (2 * j + 1) * t:(2 * j + 2) * t, :]))
    c_out = jnp.concatenate(cols, axis=-1).reshape(m, D_ATTN)

    y = _merge(x, a_out, b_out, c_out, proj(O_G, D_INP),
               wpa_ref[...], wpb_ref[...], wpc_ref[...], wout_ref[...])
    y_ref[...] = y.reshape(sb, t, D_MODEL)


def _mixer_sample(x, ck, cv, layer, cs, h0, lw, sb=32):
    nb, t, _ = x.shape
    assert nb % sb == 0 and t == 8 and (sb * t) % CHUNK == 0
    blk = lambda r, w: pl.BlockSpec((sb, r, w), lambda i: (i, 0, 0))
    cache_blk = pl.BlockSpec((sb, WINDOW, LANES), lambda i: (layer * (nb // sb) + i, 0, 0))
    consts = [lw['n1'], lw['win'], lw['wkv'], lw['cws_s'], lw['bs_s'], lw['lng'], lw['lnb'], lw['cw'], lw['cb'],
              lw['lam'], lw['wai'], lw['ba'], lw['bi'], lw['bdq'], lw['bdk'], lw['gq'], lw['gk'],
              lw['bias_s'], lw['sink_s'], lw['wpa'], lw['wpb'], lw['wpc'], lw['wout']]
    out_shape = [jax.ShapeDtypeStruct((nb, t, D_MODEL), F32),
                 jax.ShapeDtypeStruct((nb, WINDOW, LANES), F32),
                 jax.ShapeDtypeStruct((nb, WINDOW, LANES), F32),
                 jax.ShapeDtypeStruct((nb, t, D_LRU), F32),
                 jax.ShapeDtypeStruct((nb, t, D_LRU), F32),
                 jax.ShapeDtypeStruct((nb, t, D_CHUNK), F32)]
    return pl.pallas_call(
        functools.partial(_mixer_sample_kernel, sb=sb, t=t),
        grid=(nb // sb,),
        in_specs=[blk(t, D_MODEL), cache_blk, cache_blk, blk(8, D_LRU), blk(1, D_LRU)]
                 + [_resident(a.shape) for a in consts],
        out_specs=[blk(t, D_MODEL), blk(WINDOW, LANES), blk(WINDOW, LANES),
                   blk(t, D_LRU), blk(t, D_LRU), blk(t, D_CHUNK)],
        out_shape=out_shape,
        scratch_shapes=[pltpu.VMEM((sb, 8 + t, D_LRU), F32)],
        compiler_params=pltpu.CompilerParams(dimension_semantics=("parallel",),
                                             vmem_limit_bytes=VMEM_LIMIT),
        name="mixer_sample",
    )(x, ck, cv, cs, h0, *consts)


def _ffn_kernel(x_ref, g_ref, wg_ref, wu_ref, wd_ref, o_ref, *, tf):
    x = x_ref[...]
    xn = _rms(x, g_ref[...]).astype(BF16)
    acc = x
    for f in range(D_FF // tf):
        cols = slice(f * tf, (f + 1) * tf)
        gate = jnp.dot(xn, wg_ref[:, cols], preferred_element_type=F32)
        up = jnp.dot(xn, wu_ref[:, cols], preferred_element_type=F32)
        acc = acc + _mm(gate * jax.nn.sigmoid(gate) * up, wd_ref[cols, :])
    o_ref[...] = acc


def _ffn(x, g, wg, wu, wd, tm=1024, tf=512):
    n = x.shape[0]
    tm = min(tm, n)
    assert n % tm == 0 and D_FF % tf == 0
    return pl.pallas_call(
        functools.partial(_ffn_kernel, tf=tf),
        grid=(n // tm,),
        in_specs=[pl.BlockSpec((tm, D_MODEL), lambda i: (i, 0)),
                  _resident(g.shape), _resident(wg.shape), _resident(wu.shape), _resident(wd.shape)],
        out_specs=pl.BlockSpec((tm, D_MODEL), lambda i: (i, 0)),
        out_shape=jax.ShapeDtypeStruct((n, D_MODEL), F32),
        compiler_params=pltpu.CompilerParams(dimension_semantics=("parallel",),
                                             vmem_limit_bytes=VMEM_LIMIT),
        name="ffn_dense",
    )(x, g, wg, wu, wd)


def _router_kernel(x_ref, g_ref, wr_ref, tri_ref, xn_ref, comb_ref, rank_ref, combt_ref, rankt_ref, cnt_ref):
    xn = _rms(x_ref[...], g_ref[...])
    xn_ref[...] = xn.astype(BF16)
    wr = wr_ref[...]
    x_hi = xn.astype(BF16)
    x_lo = (xn - x_hi.astype(F32)).astype(BF16)
    w_hi = wr.astype(BF16)
    w_lo = (wr - w_hi.astype(F32)).astype(BF16)
    logits = (jnp.dot(x_hi, w_hi, preferred_element_type=F32) + jnp.dot(x_lo, w_hi, preferred_element_type=F32)
              + jnp.dot(x_hi, w_lo, preferred_element_type=F32))
    lane = lax.broadcasted_iota(jnp.int32, logits.shape, 1).astype(F32)
    logits = jnp.where(lane < N_EXPERTS, logits, NEG)
    m1 = jnp.max(logits, axis=-1, keepdims=True)
    i1 = jnp.min(jnp.where(logits == m1, lane, float(LANES)), axis=-1, keepdims=True)
    rest = jnp.where(lane == i1, NEG, logits)
    m2 = jnp.max(rest, axis=-1, keepdims=True)
    i2 = jnp.min(jnp.where(rest == m2, lane, float(LANES)), axis=-1, keepdims=True)
    e2 = jnp.exp(m2 - m1)
    w1 = 1.0 / (1.0 + e2)
    comb = jnp.where(lane == i1, w1, 0.0) + jnp.where(lane == i2, e2 * w1, 0.0)
    sel = jnp.where(comb > 0.0, 1.0, 0.0)
    nsub = x_ref.shape[0] // MOE_SUB_TOKENS
    selb = sel.astype(BF16)
    ranks, cnts = [], []
    for h in range(nsub):
        rows = slice(h * MOE_SUB_TOKENS, (h + 1) * MOE_SUB_TOKENS)
        ranks.append(jnp.dot(tri_ref[...], selb[rows], preferred_element_type=F32))
        cnts.append(jnp.sum(sel[rows], axis=0, keepdims=True))
    rank = jnp.concatenate(ranks, axis=0) if nsub > 1 else ranks[0]
    cnts = cnts + [jnp.zeros((8 - nsub, LANES), F32)]
    comb_ref[...] = comb
    rank_ref[...] = rank
    combt_ref[...] = comb.T[:N_EXPERTS]
    rankt_ref[...] = rank.T[:N_EXPERTS]
    cnt_ref[...] = jnp.concatenate(cnts, axis=0)[None]


def _router(x, g, wr, tri, tb):
    n = x.shape[0]
    nblk = n // tb
    row = lambda w: pl.BlockSpec((tb, w), lambda i: (i, 0))
    colb = pl.BlockSpec((N_EXPERTS, tb), lambda i: (0, i))
    return pl.pallas_call(
        _router_kernel,
        grid=(nblk,),
        in_specs=[row(D_MODEL), _resident(g.shape), _resident(wr.shape), _resident(tri.shape)],
        out_specs=[row(D_MODEL), row(LANES), row(LANES), colb, colb,
                   pl.BlockSpec((1, 8, LANES), lambda i: (i, 0, 0))],
        out_shape=[jax.ShapeDtypeStruct((n, D_MODEL), BF16),
                   jax.ShapeDtypeStruct((n, LANES), F32),
                   jax.ShapeDtypeStruct((n, LANES), F32),
                   jax.ShapeDtypeStruct((N_EXPERTS, n), F32),
                   jax.ShapeDtypeStruct((N_EXPERTS, n), F32),
                   jax.ShapeDtypeStruct((nblk, 8, LANES), F32)],
        compiler_params=pltpu.CompilerParams(dimension_semantics=("parallel",),
                                             vmem_limit_bytes=VMEM_LIMIT),
        name="moe_router",
    )(x, g, wr, tri)


def _moe_kernel(meta_ref, xn_ref, comb_ref, rank_ref, combt_ref, rankt_ref, wg_ref, wu_ref, wd_ref,
                yin_ref, yout_ref, *, tb, resident):
    nsub = tb // MOE_SUB_TOKENS
    if resident:
        e = pl.program_id(0)
        cnt0 = 1 + e * nsub
    else:
        e = meta_ref[0]
        cnt0 = 1 + pl.program_id(0) * nsub
    nmax = jnp.int32(0)
    for h in range(nsub):
        nmax = jnp.maximum(nmax, meta_ref[cnt0 + h])

    lane_e = lax.broadcasted_iota(jnp.int32, (tb, LANES), 1) == e
    cw_col = jnp.sum(jnp.where(lane_e, comb_ref[...], 0.0), axis=-1, keepdims=True)
    rk_col = jnp.sum(jnp.where(lane_e, rank_ref[...], 0.0), axis=-1, keepdims=True)
    cw_row = combt_ref[pl.ds(e, 1), :]
    rk_row = rankt_ref[pl.ds(e, 1), :]
    rk_col = jnp.where(cw_col > 0.0, rk_col, -1.0)
    rk_row = jnp.where(cw_row > 0.0, rk_row, -1.0)

    def tile(cap, row0, src_ref):
        j0 = row0.astype(F32)
        sub_i = lax.broadcasted_iota(jnp.int32, (cap, MOE_SUB_TOKENS), 0).astype(F32)
        lane_i = lax.broadcasted_iota(jnp.int32, (MOE_SUB_TOKENS, cap), 1).astype(F32)
        xg = []
        for h in range(nsub):
            tok = slice(h * MOE_SUB_TOKENS, (h + 1) * MOE_SUB_TOKENS)
            gather = jnp.where(rk_row[:, tok] - j0 == sub_i, 1.0, 0.0).astype(BF16)
            xg.append(jnp.dot(gather, xn_ref[tok, :], preferred_element_type=F32))
        xg = (jnp.concatenate(xg, axis=0) if nsub > 1 else xg[0]).astype(BF16)
        gate = jnp.dot(xg, wg_ref[0], preferred_element_type=F32)
        up = jnp.dot(xg, wu_ref[0], preferred_element_type=F32)
        o = _mm(gate * jax.nn.sigmoid(gate) * up, wd_ref[0]).astype(BF16)
        for h in range(nsub):
            tok = slice(h * MOE_SUB_TOKENS, (h + 1) * MOE_SUB_TOKENS)
            scatter = jnp.where(rk_col[tok] - j0 == lane_i, 1.0, 0.0).astype(BF16)
            part = cw_col[tok] * jnp.dot(scatter, o[h * cap:(h + 1) * cap], preferred_element_type=F32)
            yout_ref[tok, :] = src_ref[tok, :] + part

    step = MOE_PASS_ROWS[1] - MOE_PASS_ROWS[0]
    cls = jnp.clip(lax.div(nmax - MOE_PASS_ROWS[0] + (step - 1), jnp.int32(step)), 0, len(MOE_PASS_ROWS) - 1)
    if resident:
        @pl.when(e == 0)
        def _():
            yout_ref[...] = yin_ref[...]
    first_src = yout_ref if resident else yin_ref
    lax.switch(cls, [functools.partial(tile, cap, jnp.int32(0), first_src) for cap in MOE_PASS_ROWS])
    big, small = MOE_PASS_ROWS[-1], MOE_EXTRA_ROWS
    nextra = lax.div(jnp.maximum(nmax - big, 0) + (small - 1), jnp.int32(small))

    def loop_body(j, carry):
        tile(small, big + j * small, yout_ref)
        return carry

    lax.fori_loop(0, nextra, loop_body, 0)


def _moe_expert(meta, xn, comb, rank, combt, rankt, wg, wu, wd, y, tb, resident):
    n = xn.shape[0]
    if resident:
        assert n == tb
        grid = (N_EXPERTS,)
        tok_idx = lambda i, m: 0
        exp_idx = lambda i, m: i
    else:
        grid = (n // tb,)
        tok_idx = lambda i, m: i
        exp_idx = lambda i, m: m[0]
    row = lambda w: pl.BlockSpec((tb, w), lambda i, m: (tok_idx(i, m), 0))
    colb = pl.BlockSpec((N_EXPERTS, tb), lambda i, m: (0, tok_idx(i, m)))
    wspec = lambda r, w: pl.BlockSpec((1, r, w), lambda i, m: (exp_idx(i, m), 0, 0), pipeline_mode=pl.Buffered(1))
    grid_spec = pltpu.PrefetchScalarGridSpec(
        num_scalar_prefetch=1,
        grid=grid,
        in_specs=[row(D_MODEL), row(LANES), row(LANES), colb, colb,
                  wspec(D_MODEL, D_FF), wspec(D_MODEL, D_FF), wspec(D_FF, D_MODEL), row(D_MODEL)],
        out_specs=row(D_MODEL),
    )
    return pl.pallas_call(
        functools.partial(_moe_kernel, tb=tb, resident=resident),
        grid_spec=grid_spec,
        out_shape=jax.ShapeDtypeStruct((n, D_MODEL), F32),
        compiler_params=pltpu.CompilerParams(dimension_semantics=("arbitrary",),
                                             vmem_limit_bytes=VMEM_LIMIT),
        name="moe_resident" if resident else "moe_expert",
    )(meta, xn, comb, rank, combt, rankt, wg, wu, wd, y)


def _moe_ffn(x, g, wr, tri, wg, wu, wd, tb):
    xn, comb, rank, combt, rankt, cnt = _router(x, g, wr, tri, tb)
    nsub = tb // MOE_SUB_TOKENS
    cnt = jnp.transpose(cnt[:, :nsub, :N_EXPERTS].astype(jnp.int32), (2, 0, 1)).reshape(N_EXPERTS, -1)
    zero = jnp.zeros((1,), jnp.int32)
    if x.shape[0] == tb:
        return _moe_expert(jnp.concatenate([zero, cnt.reshape(-1)]), xn, comb, rank, combt, rankt,
                           wg, wu, wd, x, tb, True)
    y = x
    for e in range(N_EXPERTS):
        y = _moe_expert(jnp.concatenate([zero + e, cnt[e]]), xn, comb, rank, combt, rankt, wg, wu, wd, y, tb, False)
    return y


def _block_diag_mean(n, width):
    i = jnp.arange(n) // width
    return jnp.where(i[:, None] == i[None, :], 1.0 / width, 0.0).astype(BF16)


def _alibi_bias(nq, nk):
    dist = (WINDOW + jnp.arange(nq)[:, None] - jnp.arange(nk)[None, :]).astype(F32)
    allowed = (dist >= 0) & (dist <= WINDOW)
    slopes = jnp.exp2(-8.0 * (jnp.arange(N_HEADS, dtype=F32) + 1.0) / N_HEADS)
    return jnp.where(allowed[None], -slopes[:, None, None] * dist[None], NEG)


def _pack_layer(l, p, t_s):
    w_in = p['w_in'][l].astype(BF16)
    dup = lambda w: jnp.concatenate([w[:, i * HEAD_DIM:(i + 1) * HEAD_DIM] for i in range(N_KV_HEADS) for _ in (0, 1)], axis=1)
    s_q, s_k = 2 * D_CHUNK + 2 * D_LRU, 2 * D_CHUNK + 2 * D_LRU + D_ATTN
    s_v = s_k + N_KV_HEADS * HEAD_DIM
    s_g = s_v + N_KV_HEADS * HEAD_DIM
    assert (s_k, s_v, s_g, w_in.shape[1]) == (O_K, O_V, O_G, D_INP)
    wkv = jnp.concatenate([dup(w_in[:, s_k:s_v]), dup(w_in[:, s_v:s_g])], axis=1)
    nblk = D_LRU // p['lru_wa'].shape[-1]
    bd = lambda w: jax.scipy.linalg.block_diag(*[w[i] for i in range(nblk)])
    gw = D_CHUNK // N_CHUNK_GROUPS
    rep = CHUNK // t_s
    bias_s = _alibi_bias(t_s, WINDOW + t_s).reshape(N_HEADS * t_s, WINDOW + t_s)
    return {
        'n1': p['norm1_g'][l][None], 'win': w_in, 'wkv': wkv,
        'cws_p': p['chunk_ws'][l],
        'cws_s': jnp.tile(p['chunk_ws'][l][:, :t_s, :t_s], (1, rep, rep)),
        'bs_p': jnp.repeat(p['chunk_bs'][l].T, gw, axis=1),
        'bs_s': jnp.tile(jnp.repeat(p['chunk_bs'][l][:, :t_s].T, gw, axis=1), (rep, 1)),
        'lng': p['chunk_ln_g'][l][None], 'lnb': p['chunk_ln_b'][l][None],
        'cw': p['conv_w'][l], 'cb': p['conv_b'][l][None], 'lam': p['lru_lambda'][l][None],
        'wai': jnp.concatenate([bd(p['lru_wa'][l]), bd(p['lru_wi'][l])], axis=1).astype(BF16),
        'ba': p['lru_ba'][l][None], 'bi': p['lru_bi'][l][None],
        'bdq': _block_diag_mean(D_ATTN, HEAD_DIM), 'bdk': _block_diag_mean(2 * LANES, LANES),
        'gq': jnp.tile(p['q_norm_g'][l], N_HEADS)[None],
        'gk': jnp.tile(p['k_norm_g'][l], 2 * N_KV_HEADS)[None],
        'bias_p': _alibi_bias(CHUNK, 2 * WINDOW), 'bias_s': bias_s,
        'sinks': p['attn_sinks'][l],
        'sink_s': jnp.broadcast_to(jnp.repeat(p['attn_sinks'][l], t_s)[:, None], (N_HEADS * t_s, LANES)),
        'wpa': p['w_proj_a'][l].astype(BF16), 'wpb': p['w_proj_b'][l].astype(BF16),
        'wpc': p['w_proj_c'][l].astype(BF16), 'wout': p['w_out'][l].astype(BF16),
    }


def _strict_lower(n):
    i = jnp.arange(n)
    return jnp.where(i[None, :] < i[:, None], 1.0, 0.0).astype(BF16)


def kernel(x_prompt, x_sample, cache_win_k, cache_win_v, state_conv, state_lru_h, norm1_g, w_in, chunk_ln_g, chunk_ln_b, chunk_ws, chunk_bs, conv_w, conv_b, lru_lambda, lru_wa, lru_ba, lru_wi, lru_bi, q_norm_g, k_norm_g, attn_sinks, w_proj_a, w_proj_b, w_proj_c, w_out, norm2_g, ffn_w_gate, ffn_w_up, ffn_w_down, moe_router, moe_w_gate, moe_w_up, moe_w_down):
    p = dict(norm1_g=norm1_g, w_in=w_in, chunk_ln_g=chunk_ln_g, chunk_ln_b=chunk_ln_b, chunk_ws=chunk_ws,
             chunk_bs=chunk_bs, conv_w=conv_w, conv_b=conv_b, lru_lambda=lru_lambda, lru_wa=lru_wa,
             lru_ba=lru_ba, lru_wi=lru_wi, lru_bi=lru_bi, q_norm_g=q_norm_g, k_norm_g=k_norm_g,
             attn_sinks=attn_sinks, w_proj_a=w_proj_a, w_proj_b=w_proj_b, w_proj_c=w_proj_c, w_out=w_out)
    depth = w_in.shape[0]
    bp, tp, _ = x_prompt.shape
    bs, ts, _ = x_sample.shape
    tb_p = min(1024, bp * tp)
    tb_s = min(1024, bs * ts)
    assert tb_p % MOE_SUB_TOKENS == 0 and tb_s % MOE_SUB_TOKENS == 0
    tri = _strict_lower(MOE_SUB_TOKENS)

    yp, ys = x_prompt, x_sample
    ck = cache_win_k.reshape(depth * bs, WINDOW, LANES)
    cv = cache_win_v.reshape(depth * bs, WINDOW, LANES)
    outs = {k: [] for k in ('pk', 'pv', 'pconv', 'ph', 'sk', 'sv', 'sconv', 'sh', 'schunk')}
    for l in range(depth):
        lw = _pack_layer(l, p, ts)
        yp, pk, pv, plx, phl = _mixer_prompt(yp, lw)
        cs = jnp.pad(state_conv[l], ((0, 0), (8 - (CONV_W - 1), 0), (0, 0)))
        ys, sk, sv, slx, shl, scv = _mixer_sample(ys, ck, cv, l, cs, state_lru_h[l][:, None, :], lw)
        outs['pk'].append(pk.reshape(bp, WINDOW, N_KV_HEADS, HEAD_DIM))
        outs['pv'].append(pv.reshape(bp, WINDOW, N_KV_HEADS, HEAD_DIM))
        outs['pconv'].append(plx[:, 8 - (CONV_W - 1):, :])
        outs['ph'].append(phl[:, 0, :])
        outs['sk'].append(sk.reshape(bs, WINDOW, N_KV_HEADS, HEAD_DIM))
        outs['sv'].append(sv.reshape(bs, WINDOW, N_KV_HEADS, HEAD_DIM))
        outs['sconv'].append(slx[:, ts - (CONV_W - 1):, :])
        outs['sh'].append(shl[:, ts - 1, :])
        outs['schunk'].append(scv)

        g2 = norm2_g[l][None]
        flat_p, flat_s = yp.reshape(bp * tp, D_MODEL), ys.reshape(bs * ts, D_MODEL)
        j = l // 2
        if l % 2 == 0:
            wg, wu, wd = ffn_w_gate[j].astype(BF16), ffn_w_up[j].astype(BF16), ffn_w_down[j].astype(BF16)
            flat_p = _ffn(flat_p, g2, wg, wu, wd)
            flat_s = _ffn(flat_s, g2, wg, wu, wd)
        else:
            wg, wu, wd = moe_w_gate[j].astype(BF16), moe_w_up[j].astype(BF16), moe_w_down[j].astype(BF16)
            wr = jnp.pad(moe_router[j], ((0, 0), (0, LANES - N_EXPERTS)))
            flat_p = _moe_ffn(flat_p, g2, wr, tri, wg, wu, wd, tb_p)
            flat_s = _moe_ffn(flat_s, g2, wr, tri, wg, wu, wd, tb_s)
        yp, ys = flat_p.reshape(bp, tp, D_MODEL), flat_s.reshape(bs, ts, D_MODEL)

    st = {k: jnp.stack(v) for k, v in outs.items()}
    return (yp, ys, st['pk'], st['pv'], st['pconv'], st['ph'],
            st['sk'], st['sv'], st['sconv'], st['sh'], st['schunk'])
```

```python
import functools
import math

import jax
import jax.numpy as jnp
from jax import lax
from jax.experimental import pallas as pl
from jax.experimental.pallas import tpu as pltpu

F32 = jnp.float32
BF16 = jnp.bfloat16

D_MODEL = 1024
D_CHUNK = 512
CHUNK = 128
N_CHUNK_GROUPS = 4
D_LRU = 512
CONV_W = 4
LRU_C = 8.0
N_HEADS = 8
N_KV_HEADS = 2
HEAD_DIM = 64
D_ATTN = N_HEADS * HEAD_DIM
WINDOW = 128
D_FF = 3584
N_EXPERTS = 8
EPS = 1e-6
NEG = -1e30

O_UV = 0
O_LX = 2 * D_CHUNK
O_LG = O_LX + D_LRU
O_Q = O_LG + D_LRU
O_K = O_Q + D_ATTN
O_V = O_K + N_KV_HEADS * HEAD_DIM
O_G = O_V + N_KV_HEADS * HEAD_DIM
D_INP = O_G + 3 * D_MODEL
W_KK = 2 * N_KV_HEADS * HEAD_DIM

LANES = 128
MOE_SUB_TOKENS = 512
MOE_PASS_ROWS = (96, 128, 160, 192)
MOE_EXTRA_ROWS = 48
VMEM_LIMIT = 56 * 1024 * 1024


def _resident(shape):
    nd = len(shape)
    return pl.BlockSpec(shape, lambda *_: (0,) * nd, pipeline_mode=pl.Buffered(1))


def _mm(a, w):
    return jnp.dot(a.astype(BF16), w, preferred_element_type=F32)


def _gelu(x):
    return 0.5 * x * (1.0 + jnp.tanh(0.7978845608028654 * (x + 0.044715 * (x * x * x))))


def _rms(x, g):
    return x * lax.rsqrt(jnp.mean(x * x, axis=-1, keepdims=True) + EPS) * g


def _seg_scan(a, b, seg):
    pos = lax.broadcasted_iota(jnp.int32, (a.shape[0], 1), 0) & (seg - 1)
    s = 1
    while s < seg:
        ok = pos >= s
        a_sh = pltpu.roll(a, s, 0)
        b_sh = pltpu.roll(b, s, 0)
        b = jnp.where(ok, a * b_sh + b, b)
        a = jnp.where(ok, a * a_sh, a)
        s *= 2
    return a, b


def _chunk_branch(uv, wm, bs_full, ln_g, ln_b):
    uv = _gelu(uv)
    u = uv[:, :D_CHUNK]
    v = uv[:, D_CHUNK:]
    mu = jnp.mean(v, axis=-1, keepdims=True)
    vc = v - mu
    v = vc * lax.rsqrt(jnp.mean(vc * vc, axis=-1, keepdims=True) + EPS) * ln_g + ln_b
    vb = v.astype(BF16)
    nblk = uv.shape[0] // CHUNK
    gw = D_CHUNK // N_CHUNK_GROUPS
    rows = []
    for r in range(nblk):
        cols = []
        for g in range(N_CHUNK_GROUPS):
            vg = vb[r * CHUNK:(r + 1) * CHUNK, g * gw:(g + 1) * gw]
            cols.append(jnp.dot(wm[g], vg, preferred_element_type=F32))
        rows.append(jnp.concatenate(cols, axis=-1) + bs_full)
    s = jnp.concatenate(rows, axis=0) if nblk > 1 else rows[0]
    return u * s, v


def _lru_inputs(xc, lam, w_ai, b_a, b_i):
    ri = _mm(xc, w_ai)
    r = jax.nn.sigmoid(ri[:, :D_LRU] + b_a)
    i = jax.nn.sigmoid(ri[:, D_LRU:] + b_i)
    log_sig = jnp.minimum(lam, 0.0) - jnp.log1p(jnp.exp(-jnp.abs(lam)))
    log_a = LRU_C * r * log_sig
    a = jnp.exp(log_a)
    t = jnp.tanh(log_a)
    one_m_a2 = -2.0 * t / (1.0 - t)
    return a, jnp.sqrt(one_m_a2) * (i * xc)


def _conv(taps, conv_w, conv_b):
    out = conv_b + conv_w[3:4] * taps[0]
    for d in range(1, CONV_W):
        out = out + conv_w[3 - d:4 - d] * taps[d]
    return out


def _head_norm(x, bd, g):
    ms = jnp.dot((x * x).astype(BF16), bd, preferred_element_type=F32)
    return x * lax.rsqrt(ms + EPS) * g


def _merge(x, a_out, b_out, c_out, gl, wpa, wpb, wpc, wout):
    gates = jax.nn.sigmoid(gl)
    merged = (gates[:, :D_MODEL] * _mm(a_out, wpa)
              + gates[:, D_MODEL:2 * D_MODEL] * _mm(b_out, wpb)
              + gates[:, 2 * D_MODEL:] * _mm(c_out, wpc))
    return x + _mm(merged, wout)


def _mixer_prompt_kernel(sinks_ref, x_ref, n1_ref, win_ref, wkv_ref, cws_ref, bs_ref, lng_ref, lnb_ref,
                         cw_ref, cb_ref, lam_ref, wai_ref, ba_ref, bi_ref, bdq_ref, bdk_ref,
                         gq_ref, gk_ref, bias_ref, wpa_ref, wpb_ref, wpc_ref, wout_ref,
                         y_ref, kk_ref, vk_ref, lx_ref, h_ref,
                         xpad_ref, hc_ref, pk_ref, pv_ref, *, bb):
    c = pl.program_id(1)
    m = bb * CHUNK

    @pl.when(c == 0)
    def _():
        xpad_ref[:, 0:8, :] = jnp.zeros((bb, 8, D_LRU), F32)
        hc_ref[...] = jnp.zeros_like(hc_ref)
        pk_ref[...] = jnp.zeros_like(pk_ref)
        pv_ref[...] = jnp.zeros_like(pv_ref)

    x = x_ref[...].reshape(m, D_MODEL)
    xn = _rms(x, n1_ref[...]).astype(BF16)

    def proj(lo, hi):
        return jnp.dot(xn, win_ref[:, lo:hi], preferred_element_type=F32)

    ri = lax.broadcasted_iota(jnp.int32, (CHUNK, CHUNK), 0)
    ci = lax.broadcasted_iota(jnp.int32, (CHUNK, CHUNK), 1)
    wm = [jnp.where(ri >= ci, cws_ref[g], 0.0).astype(BF16) for g in range(N_CHUNK_GROUPS)]
    a_out, _ = _chunk_branch(proj(O_UV, O_LX), wm, bs_ref[...], lng_ref[...], lnb_ref[...])

    lx = proj(O_LX, O_LG)
    xpad_ref[:, 8:8 + CHUNK, :] = lx.reshape(bb, CHUNK, D_LRU)
    taps = [lx] + [xpad_ref[:, 8 - d:8 - d + CHUNK, :].reshape(m, D_LRU) for d in range(1, CONV_W)]
    xc = _conv(taps, cw_ref[...], cb_ref[...])
    a, b = _lru_inputs(xc, lam_ref[...], wai_ref[...], ba_ref[...], bi_ref[...])
    a, b = _seg_scan(a, b, 8)
    hs = []
    for r in range(bb):
        carry = hc_ref[r]
        for k in range(CHUNK // 8):
            rows = slice(r * CHUNK + 8 * k, r * CHUNK + 8 * k + 8)
            h_t = a[rows] * carry + b[rows]
            carry = h_t[7:8]
            hs.append(h_t)
        hc_ref[r] = carry
    hs = jnp.concatenate(hs, axis=0)
    b_out = _gelu(proj(O_LG, O_Q)) * hs
    tail = xpad_ref[:, CHUNK:CHUNK + 8, :]
    xpad_ref[:, 0:8, :] = tail
    lx_ref[...] = tail
    h_ref[...] = hc_ref[...]

    qn = _head_norm(proj(O_Q, O_K), bdq_ref[...], gq_ref[...]) * (HEAD_DIM ** -0.5)
    kkn = _head_norm(jnp.dot(xn, wkv_ref[:, :W_KK], preferred_element_type=F32), bdk_ref[...], gk_ref[...])
    vv = jnp.dot(xn, wkv_ref[:, W_KK:], preferred_element_type=F32)
    lane = lax.broadcasted_iota(jnp.int32, (1, LANES), 1)
    lane_lo = lane < HEAD_DIM
    kcol = lax.broadcasted_iota(jnp.int32, (1, 2 * WINDOW), 1)
    pen = jnp.where(kcol < WINDOW, jnp.where(c == 0, NEG, 0.0), 0.0)
    c_rows = []
    for r in range(bb):
        rows = slice(r * CHUNK, (r + 1) * CHUNK)
        kcat = jnp.concatenate([pk_ref[r], kkn[rows]], axis=0).astype(BF16)
        vcat = jnp.concatenate([pv_ref[r], vv[rows]], axis=0).astype(BF16)
        cols = []
        for j in range(N_HEADS // 2):
            kv = j // 2
            qj = qn[rows, j * LANES:(j + 1) * LANES]
            kh = kcat[:, kv * LANES:(kv + 1) * LANES]
            vh = vcat[:, kv * LANES:(kv + 1) * LANES]
            outs = []
            for half in range(2):
                hd = 2 * j + half
                qm = jnp.where(lane_lo if half == 0 else jnp.logical_not(lane_lo), qj, 0.0).astype(BF16)
                lg = lax.dot_general(qm, kh, (((1,), (1,)), ((), ())), preferred_element_type=F32)
                lg = lg + bias_ref[hd] + pen
                sink = sinks_ref[hd]
                mx = jnp.maximum(jnp.max(lg, axis=-1, keepdims=True), sink)
                p = jnp.exp(lg - mx)
                den = jnp.sum(p, axis=-1, keepdims=True) + jnp.exp(sink - mx)
                outs.append(jnp.dot(p.astype(BF16), vh, preferred_element_type=F32) / den)
            cols.append(jnp.where(lane_lo, outs[0], outs[1]))
        c_rows.append(jnp.concatenate(cols, axis=-1))
        pk_ref[r] = kkn[rows]
        pv_ref[r] = vv[rows]
    c_out = jnp.concatenate(c_rows, axis=0) if bb > 1 else c_rows[0]
    kk_ref[...] = jnp.where(lane_lo, kkn[:, :LANES], kkn[:, LANES:]).reshape(bb, CHUNK, LANES)
    vk_ref[...] = jnp.where(lane_lo, vv[:, :LANES], vv[:, LANES:]).reshape(bb, CHUNK, LANES)

    y = _merge(x, a_out, b_out, c_out, proj(O_G, D_INP),
               wpa_ref[...], wpb_ref[...], wpc_ref[...], wout_ref[...])
    y_ref[...] = y.reshape(bb, CHUNK, D_MODEL)


def _mixer_prompt(x, lw, bb=4):
    bsz, t, _ = x.shape
    nc = t // CHUNK
    assert t % CHUNK == 0 and bsz % bb == 0
    seq_blk = lambda w: pl.BlockSpec((bb, CHUNK, w), lambda i, c: (i, c, 0))
    keep_blk = lambda r, w: pl.BlockSpec((bb, r, w), lambda i, c: (i, 0, 0))
    consts = [lw['n1'], lw['win'], lw['wkv'], lw['cws_p'], lw['bs_p'], lw['lng'], lw['lnb'], lw['cw'], lw['cb'],
              lw['lam'], lw['wai'], lw['ba'], lw['bi'], lw['bdq'], lw['bdk'], lw['gq'], lw['gk'],
              lw['bias_p'], lw['wpa'], lw['wpb'], lw['wpc'], lw['wout']]
    out_shape = [jax.ShapeDtypeStruct((bsz, t, D_MODEL), F32),
                 jax.ShapeDtypeStruct((bsz, WINDOW, LANES), F32),
                 jax.ShapeDtypeStruct((bsz, WINDOW, LANES), F32),
                 jax.ShapeDtypeStruct((bsz, 8, D_LRU), F32),
                 jax.ShapeDtypeStruct((bsz, 1, D_LRU), F32)]
    return pl.pallas_call(
        functools.partial(_mixer_prompt_kernel, bb=bb),
        grid=(bsz // bb, nc),
        in_specs=[pl.BlockSpec(memory_space=pltpu.SMEM), seq_blk(D_MODEL)] + [_resident(a.shape) for a in consts],
        out_specs=[seq_blk(D_MODEL), keep_blk(WINDOW, LANES), keep_blk(WINDOW, LANES),
                   keep_blk(8, D_LRU), keep_blk(1, D_LRU)],
        out_shape=out_shape,
        scratch_shapes=[pltpu.VMEM((bb, CHUNK + 8, D_LRU), F32),
                        pltpu.VMEM((bb, 1, D_LRU), F32),
                        pltpu.VMEM((bb, CHUNK, 2 * LANES), F32),
                        pltpu.VMEM((bb, CHUNK, 2 * LANES), F32)],
        compiler_params=pltpu.CompilerParams(dimension_semantics=("parallel", "arbitrary"),
                                             vmem_limit_bytes=VMEM_LIMIT),
        name="mixer_prompt",
    )(lw['sinks'], x, *consts)


def _mixer_sample_kernel(x_ref, ck_ref, cv_ref, cs_ref, h0_ref, n1_ref, win_ref, wkv_ref, cws_ref, bs_ref,
                         lng_ref, lnb_ref, cw_ref, cb_ref, lam_ref, wai_ref, ba_ref, bi_ref,
                         bdq_ref, bdk_ref, gq_ref, gk_ref, bias_ref, sink_ref,
                         wpa_ref, wpb_ref, wpc_ref, wout_ref,
                         y_ref, kk_ref, vk_ref, lx_ref, h_ref, cvn_ref,
                         xpad_ref, *, sb, t):
    m = sb * t

    x = x_ref[...].reshape(m, D_MODEL)
    xn = _rms(x, n1_ref[...]).astype(BF16)

    def proj(lo, hi):
        return jnp.dot(xn, win_ref[:, lo:hi], preferred_element_type=F32)

    ri = lax.broadcasted_iota(jnp.int32, (CHUNK, CHUNK), 0)
    ci = lax.broadcasted_iota(jnp.int32, (CHUNK, CHUNK), 1)
    keep = ((ri // t) == (ci // t)) & ((ci % t) <= (ri % t))
    wm = [jnp.where(keep, cws_ref[g], 0.0).astype(BF16) for g in range(N_CHUNK_GROUPS)]
    a_out, v_norm = _chunk_branch(proj(O_UV, O_LX), wm, bs_ref[...], lng_ref[...], lnb_ref[...])
    cvn_ref[...] = v_norm.reshape(sb, t, D_CHUNK)

    lx = proj(O_LX, O_LG)
    xpad_ref[:, 0:8, :] = cs_ref[...]
    xpad_ref[:, 8:8 + t, :] = lx.reshape(sb, t, D_LRU)
    taps = [lx] + [xpad_ref[:, 8 - d:8 - d + t, :].reshape(m, D_LRU) for d in range(1, CONV_W)]
    xc = _conv(taps, cw_ref[...], cb_ref[...])
    a, b = _lru_inputs(xc, lam_ref[...], wai_ref[...], ba_ref[...], bi_ref[...])
    a, b = _seg_scan(a, b, t)
    h0 = jnp.broadcast_to(h0_ref[...], (sb, t, D_LRU)).reshape(m, D_LRU)
    hs = a * h0 + b
    b_out = _gelu(proj(O_LG, O_Q)) * hs
    lx_ref[...] = lx.reshape(sb, t, D_LRU)
    h_ref[...] = hs.reshape(sb, t, D_LRU)

    qn = _head_norm(proj(O_Q, O_K), bdq_ref[...], gq_ref[...]) * (HEAD_DIM ** -0.5)
    kkn = _head_norm(jnp.dot(xn, wkv_ref[:, :W_KK], preferred_element_type=F32), bdk_ref[...], gk_ref[...])
    vv = jnp.dot(xn, wkv_ref[:, W_KK:], preferred_element_type=F32)
    lane = lax.broadcasted_iota(jnp.int32, (1, 1, LANES), 2)
    lane_lo = lane < HEAD_DIM
    lane_lo2 = lane_lo.reshape(1, LANES)
    knew = jnp.where(lane_lo2, kkn[:, :LANES], kkn[:, LANES:]).reshape(sb, t, LANES)
    vnew = jnp.where(lane_lo2, vv[:, :LANES], vv[:, LANES:]).reshape(sb, t, LANES)
    kc = jnp.concatenate([ck_ref[...], knew], axis=1)
    vc = jnp.concatenate([cv_ref[...], vnew], axis=1)
    kk_ref[...] = kc[:, t:, :]
    vk_ref[...] = vc[:, t:, :]
    q3 = qn.reshape(sb, t, D_ATTN)
    q3r = pltpu.roll(qn, HEAD_DIM, 1).reshape(sb, t, D_ATTN)
    pieces = []
    for hd in range(N_HEADS):
        kv = hd // (N_HEADS // N_KV_HEADS)
        if hd % 2 == kv:
            src, col = q3, hd // 2
        else:
            src, col = q3r, (hd + 1) // 2
        msk = lane_lo if kv == 0 else jnp.logical_not(lane_lo)
        pieces.append(jnp.where(msk, src[:, :, col * LANES:(col + 1) * LANES], 0.0))
    qs = jnp.concatenate(pieces, axis=1).astype(BF16)
    lg = jnp.einsum('bqd,bkd->bqk', qs, kc.astype(BF16), preferred_element_type=F32)
    lg = lg + bias_ref[...]
    sink = sink_ref[:, 0:1]
    mx = jnp.maximum(jnp.max(lg, axis=-1, keepdims=True), sink)
    p = jnp.exp(lg - mx)
    den = jnp.sum(p, axis=-1, keepdims=True) + jnp.exp(sink - mx)
    o = jnp.einsum('bqk,bkd->bqd', p.astype(BF16), vc.astype(BF16), preferred_element_type=F32) / den
    o_r = pltpu.roll(o.reshape(sb * N_HEADS * t, LANES), HEAD_DIM, 1).reshape(sb, N_HEADS * t, LANES)
    cols = []
    for j in range(N_HEADS // 2):
        kv = j // 2
        lo_src = o if kv == 0 else o_r
        hi_src = o_r if kv == 0 else o
        cols.append(jnp.where(lane_lo, lo_src[:, 2 * j * t:(2 * j + 1) * t, :],
                              hi_src[:, (2 * j + 1) * t:(2 * j + 2) * t, :]))
    c_out = jnp.concatenate(cols, axis=-1).reshape(m, D_ATTN)

    y = _merge(x, a_out, b_out, c_out, proj(O_G, D_INP),
               wpa_ref[...], wpb_ref[...], wpc_ref[...], wout_ref[...])
    y_ref[...] = y.reshape(sb, t, D_MODEL)


def _mixer_sample(x, ck, cv, layer, cs, h0, lw, sb=32):
    nb, t, _ = x.shape
    assert nb % sb == 0 and t == 8 and (sb * t) % CHUNK == 0
    blk = lambda r, w: pl.BlockSpec((sb, r, w), lambda i: (i, 0, 0))
    cache_blk = pl.BlockSpec((sb, WINDOW, LANES), lambda i: (layer * (nb // sb) + i, 0, 0))
    consts = [lw['n1'], lw['win'], lw['wkv'], lw['cws_s'], lw['bs_s'], lw['lng'], lw['lnb'], lw['cw'], lw['cb'],
              lw['lam'], lw['wai'], lw['ba'], lw['bi'], lw['bdq'], lw['bdk'], lw['gq'], lw['gk'],
              lw['bias_s'], lw['sink_s'], lw['wpa'], lw['wpb'], lw['wpc'], lw['wout']]
    out_shape = [jax.ShapeDtypeStruct((nb, t, D_MODEL), F32),
                 jax.ShapeDtypeStruct((nb, WINDOW, LANES), F32),
                 jax.ShapeDtypeStruct((nb, WINDOW, LANES), F32),
                 jax.ShapeDtypeStruct((nb, t, D_LRU), F32),
                 jax.ShapeDtypeStruct((nb, t, D_LRU), F32),
                 jax.ShapeDtypeStruct((nb, t, D_CHUNK), F32)]
    return pl.pallas_call(
        functools.partial(_mixer_sample_kernel, sb=sb, t=t),
        grid=(nb // sb,),
        in_specs=[blk(t, D_MODEL), cache_blk, cache_blk, blk(8, D_LRU), blk(1, D_LRU)]
                 + [_resident(a.shape) for a in consts],
        out_specs=[blk(t, D_MODEL), blk(WINDOW, LANES), blk(WINDOW, LANES),
                   blk(t, D_LRU), blk(t, D_LRU), blk(t, D_CHUNK)],
        out_shape=out_shape,
        scratch_shapes=[pltpu.VMEM((sb, 8 + t, D_LRU), F32)],
        compiler_params=pltpu.CompilerParams(dimension_semantics=("parallel",),
                                             vmem_limit_bytes=VMEM_LIMIT),
        name="mixer_sample",
    )(x, ck, cv, cs, h0, *consts)


def _ffn_kernel(x_ref, g_ref, wg_ref, wu_ref, wd_ref, o_ref, *, tf):
    x = x_ref[...]
    xn = _rms(x, g_ref[...]).astype(BF16)
    acc = x
    for f in range(D_FF // tf):
        cols = slice(f * tf, (f + 1) * tf)
        gate = jnp.dot(xn, wg_ref[:, cols], preferred_element_type=F32)
        up = jnp.dot(xn, wu_ref[:, cols], preferred_element_type=F32)
        acc = acc + _mm(gate * (0.5 * jnp.tanh(0.5 * gate) + 0.5) * up, wd_ref[cols, :])
    o_ref[...] = acc


def _ffn(x, g, wg, wu, wd, tm=1024, tf=512):
    n = x.shape[0]
    tm = min(tm, n)
    assert n % tm == 0 and D_FF % tf == 0
    return pl.pallas_call(
        functools.partial(_ffn_kernel, tf=tf),
        grid=(n // tm,),
        in_specs=[pl.BlockSpec((tm, D_MODEL), lambda i: (i, 0)),
                  _resident(g.shape), _resident(wg.shape), _resident(wu.shape), _resident(wd.shape)],
        out_specs=pl.BlockSpec((tm, D_MODEL), lambda i: (i, 0)),
        out_shape=jax.ShapeDtypeStruct((n, D_MODEL), F32),
        compiler_params=pltpu.CompilerParams(dimension_semantics=("parallel",),
                                             vmem_limit_bytes=VMEM_LIMIT),
        name="ffn_dense",
    )(x, g, wg, wu, wd)


def _router_kernel(x_ref, g_ref, wr_ref, tri_ref, xn_ref, comb_ref, rank_ref, combt_ref, rankt_ref, cnt_ref):
    xn = _rms(x_ref[...], g_ref[...])
    xn_ref[...] = xn.astype(BF16)
    wr = wr_ref[...]
    x_hi = xn.astype(BF16)
    x_lo = (xn - x_hi.astype(F32)).astype(BF16)
    w_hi = wr.astype(BF16)
    w_lo = (wr - w_hi.astype(F32)).astype(BF16)
    logits = (jnp.dot(x_hi, w_hi, preferred_element_type=F32) + jnp.dot(x_lo, w_hi, preferred_element_type=F32)
              + jnp.dot(x_hi, w_lo, preferred_element_type=F32))
    lane = lax.broadcasted_iota(jnp.int32, logits.shape, 1).astype(F32)
    logits = jnp.where(lane < N_EXPERTS, logits, NEG)
    m1 = jnp.max(logits, axis=-1, keepdims=True)
    i1 = jnp.min(jnp.where(logits == m1, lane, float(LANES)), axis=-1, keepdims=True)
    rest = jnp.where(lane == i1, NEG, logits)
    m2 = jnp.max(rest, axis=-1, keepdims=True)
    i2 = jnp.min(jnp.where(rest == m2, lane, float(LANES)), axis=-1, keepdims=True)
    e2 = jnp.exp(m2 - m1)
    w1 = 1.0 / (1.0 + e2)
    comb = jnp.where(lane == i1, w1, 0.0) + jnp.where(lane == i2, e2 * w1, 0.0)
    sel = jnp.where(comb > 0.0, 1.0, 0.0)
    nsub = x_ref.shape[0] // MOE_SUB_TOKENS
    selb = sel.astype(BF16)
    ranks, cnts = [], []
    for h in range(nsub):
        rows = slice(h * MOE_SUB_TOKENS, (h + 1) * MOE_SUB_TOKENS)
        ranks.append(jnp.dot(tri_ref[...], selb[rows], preferred_element_type=F32))
        cnts.append(jnp.sum(sel[rows], axis=0, keepdims=True))
    rank = jnp.concatenate(ranks, axis=0) if nsub > 1 else ranks[0]
    cnts = cnts + [jnp.zeros((8 - nsub, LANES), F32)]
    comb_ref[...] = comb
    rank_ref[...] = rank
    combt_ref[...] = comb.T[:N_EXPERTS]
    rankt_ref[...] = rank.T[:N_EXPERTS]
    cnt_ref[...] = jnp.concatenate(cnts, axis=0)[None]


def _router(x, g, wr, tri, tb):
    n = x.shape[0]
    nblk = n // tb
    row = lambda w: pl.BlockSpec((tb, w), lambda i: (i, 0))
    colb = pl.BlockSpec((N_EXPERTS, tb), lambda i: (0, i))
    return pl.pallas_call(
        _router_kernel,
        grid=(nblk,),
        in_specs=[row(D_MODEL), _resident(g.shape), _resident(wr.shape), _resident(tri.shape)],
        out_specs=[row(D_MODEL), row(LANES), row(LANES), colb, colb,
                   pl.BlockSpec((1, 8, LANES), lambda i: (i, 0, 0))],
        out_shape=[jax.ShapeDtypeStruct((n, D_MODEL), BF16),
                   jax.ShapeDtypeStruct((n, LANES), F32),
                   jax.ShapeDtypeStruct((n, LANES), F32),
                   jax.ShapeDtypeStruct((N_EXPERTS, n), F32),
                   jax.ShapeDtypeStruct((N_EXPERTS, n), F32),
                   jax.ShapeDtypeStruct((nblk, 8, LANES), F32)],
        compiler_params=pltpu.CompilerParams(dimension_semantics=("parallel",),
                                             vmem_limit_bytes=VMEM_LIMIT),
        name="moe_router",
    )(x, g, wr, tri)


def _moe_kernel(meta_ref, xn_ref, comb_ref, rank_ref, combt_ref, rankt_ref, wg_ref, wu_ref, wd_ref,
                yin_ref, yout_ref, *, tb, resident):
    nsub = tb // MOE_SUB_TOKENS
    if resident:
        e = pl.program_id(0)
        cnt0 = 1 + e * nsub
    else:
        e = meta_ref[0]
        cnt0 = 1 + pl.program_id(0) * nsub
    nmax = jnp.int32(0)
    for h in range(nsub):
        nmax = jnp.maximum(nmax, meta_ref[cnt0 + h])

    lane_e = lax.broadcasted_iota(jnp.int32, (tb, LANES), 1) == e
    cw_col = jnp.sum(jnp.where(lane_e, comb_ref[...], 0.0), axis=-1, keepdims=True)
    rk_col = jnp.sum(jnp.where(lane_e, rank_ref[...], 0.0), axis=-1, keepdims=True)
    cw_row = combt_ref[pl.ds(e, 1), :]
    rk_row = rankt_ref[pl.ds(e, 1), :]
    rk_col = jnp.where(cw_col > 0.0, rk_col, -1.0)
    rk_row = jnp.where(cw_row > 0.0, rk_row, -1.0)

    def tile(cap, row0, src_ref):
        j0 = row0.astype(F32)
        sub_i = lax.broadcasted_iota(jnp.int32, (cap, MOE_SUB_TOKENS), 0).astype(F32)
        lane_i = lax.broadcasted_iota(jnp.int32, (MOE_SUB_TOKENS, cap), 1).astype(F32)
        xg = []
        for h in range(nsub):
            tok = slice(h * MOE_SUB_TOKENS, (h + 1) * MOE_SUB_TOKENS)
            gather = jnp.where(rk_row[:, tok] - j0 == sub_i, 1.0, 0.0).astype(BF16)
            xg.append(jnp.dot(gather, xn_ref[tok, :], preferred_element_type=F32))
        xg = (jnp.concatenate(xg, axis=0) if nsub > 1 else xg[0]).astype(BF16)
        gate = jnp.dot(xg, wg_ref[0], preferred_element_type=F32)
        up = jnp.dot(xg, wu_ref[0], preferred_element_type=F32)
        o = _mm(gate * jax.nn.sigmoid(gate) * up, wd_ref[0]).astype(BF16)
        for h in range(nsub):
            tok = slice(h * MOE_SUB_TOKENS, (h + 1) * MOE_SUB_TOKENS)
            scatter = jnp.where(rk_col[tok] - j0 == lane_i, 1.0, 0.0).astype(BF16)
            part = cw_col[tok] * jnp.dot(scatter, o[h * cap:(h + 1) * cap], preferred_element_type=F32)
            yout_ref[tok, :] = src_ref[tok, :] + part

    step = MOE_PASS_ROWS[1] - MOE_PASS_ROWS[0]
    cls = jnp.clip(lax.div(nmax - MOE_PASS_ROWS[0] + (step - 1), jnp.int32(step)), 0, len(MOE_PASS_ROWS) - 1)
    if resident:
        @pl.when(e == 0)
        def _():
            yout_ref[...] = yin_ref[...]
    first_src = yout_ref if resident else yin_ref
    lax.switch(cls, [functools.partial(tile, cap, jnp.int32(0), first_src) for cap in MOE_PASS_ROWS])
    big, small = MOE_PASS_ROWS[-1], MOE_EXTRA_ROWS
    nextra = lax.div(jnp.maximum(nmax - big, 0) + (small - 1), jnp.int32(small))

    def loop_body(j, carry):
        tile(small, big + j * small, yout_ref)
        return carry

    lax.fori_loop(0, nextra, loop_body, 0)


def _moe_expert(meta, xn, comb, rank, combt, rankt, wg, wu, wd, y, tb, resident):
    n = xn.shape[0]
    if resident:
        assert n == tb
        grid = (N_EXPERTS,)
        tok_idx = lambda i, m: 0
        exp_idx = lambda i, m: i
    else:
        grid = (n // tb,)
        tok_idx = lambda i, m: i
        exp_idx = lambda i, m: m[0]
    row = lambda w: pl.BlockSpec((tb, w), lambda i, m: (tok_idx(i, m), 0))
    colb = pl.BlockSpec((N_EXPERTS, tb), lambda i, m: (0, tok_idx(i, m)))
    wspec = lambda r, w: pl.BlockSpec((1, r, w), lambda i, m: (exp_idx(i, m), 0, 0), pipeline_mode=pl.Buffered(1))
    grid_spec = pltpu.PrefetchScalarGridSpec(
        num_scalar_prefetch=1,
        grid=grid,
        in_specs=[row(D_MODEL), row(LANES), row(LANES), colb, colb,
                  wspec(D_MODEL, D_FF), wspec(D_MODEL, D_FF), wspec(D_FF, D_MODEL), row(D_MODEL)],
        out_specs=row(D_MODEL),
    )
    return pl.pallas_call(
        functools.partial(_moe_kernel, tb=tb, resident=resident),
        grid_spec=grid_spec,
        out_shape=jax.ShapeDtypeStruct((n, D_MODEL), F32),
        compiler_params=pltpu.CompilerParams(dimension_semantics=("arbitrary",),
                                             vmem_limit_bytes=VMEM_LIMIT),
        name="moe_resident" if resident else "moe_expert",
    )(meta, xn, comb, rank, combt, rankt, wg, wu, wd, y)


def _moe_ffn(x, g, wr, tri, wg, wu, wd, tb):
    xn, comb, rank, combt, rankt, cnt = _router(x, g, wr, tri, tb)
    nsub = tb // MOE_SUB_TOKENS
    cnt = jnp.transpose(cnt[:, :nsub, :N_EXPERTS].astype(jnp.int32), (2, 0, 1)).reshape(N_EXPERTS, -1)
    zero = jnp.zeros((1,), jnp.int32)
    if x.shape[0] == tb:
        return _moe_expert(jnp.concatenate([zero, cnt.reshape(-1)]), xn, comb, rank, combt, rankt,
                           wg, wu, wd, x, tb, True)
    y = x
    for e in range(N_EXPERTS):
        y = _moe_expert(jnp.concatenate([zero + e, cnt[e]]), xn, comb, rank, combt, rankt, wg, wu, wd, y, tb, False)
    return y


def _block_diag_mean(n, width):
    i = jnp.arange(n) // width
    return jnp.where(i[:, None] == i[None, :], 1.0 / width, 0.0).astype(BF16)


def _alibi_bias(nq, nk):
    dist = (WINDOW + jnp.arange(nq)[:, None] - jnp.arange(nk)[None, :]).astype(F32)
    allowed = (dist >= 0) & (dist <= WINDOW)
    slopes = jnp.exp2(-8.0 * (jnp.arange(N_HEADS, dtype=F32) + 1.0) / N_HEADS)
    return jnp.where(allowed[None], -slopes[:, None, None] * dist[None], NEG)


def _pack_layer(l, p, t_s):
    w_in = p['w_in'][l].astype(BF16)
    dup = lambda w: jnp.concatenate([w[:, i * HEAD_DIM:(i + 1) * HEAD_DIM] for i in range(N_KV_HEADS) for _ in (0, 1)], axis=1)
    s_q, s_k = 2 * D_CHUNK + 2 * D_LRU, 2 * D_CHUNK + 2 * D_LRU + D_ATTN
    s_v = s_k + N_KV_HEADS * HEAD_DIM
    s_g = s_v + N_KV_HEADS * HEAD_DIM
    assert (s_k, s_v, s_g, w_in.shape[1]) == (O_K, O_V, O_G, D_INP)
    wkv = jnp.concatenate([dup(w_in[:, s_k:s_v]), dup(w_in[:, s_v:s_g])], axis=1)
    nblk = D_LRU // p['lru_wa'].shape[-1]
    bd = lambda w: jax.scipy.linalg.block_diag(*[w[i] for i in range(nblk)])
    gw = D_CHUNK // N_CHUNK_GROUPS
    rep = CHUNK // t_s
    bias_s = _alibi_bias(t_s, WINDOW + t_s).reshape(N_HEADS * t_s, WINDOW + t_s)
    return {
        'n1': p['norm1_g'][l][None], 'win': w_in, 'wkv': wkv,
        'cws_p': p['chunk_ws'][l],
        'cws_s': jnp.tile(p['chunk_ws'][l][:, :t_s, :t_s], (1, rep, rep)),
        'bs_p': jnp.repeat(p['chunk_bs'][l].T, gw, axis=1),
        'bs_s': jnp.tile(jnp.repeat(p['chunk_bs'][l][:, :t_s].T, gw, axis=1), (rep, 1)),
        'lng': p['chunk_ln_g'][l][None], 'lnb': p['chunk_ln_b'][l][None],
        'cw': p['conv_w'][l], 'cb': p['conv_b'][l][None], 'lam': p['lru_lambda'][l][None],
        'wai': jnp.concatenate([bd(p['lru_wa'][l]), bd(p['lru_wi'][l])], axis=1).astype(BF16),
        'ba': p['lru_ba'][l][None], 'bi': p['lru_bi'][l][None],
        'bdq': _block_diag_mean(D_ATTN, HEAD_DIM), 'bdk': _block_diag_mean(2 * LANES, LANES),
        'gq': jnp.tile(p['q_norm_g'][l], N_HEADS)[None],
        'gk': jnp.tile(p['k_norm_g'][l], 2 * N_KV_HEADS)[None],
        'bias_p': _alibi_bias(CHUNK, 2 * WINDOW), 'bias_s': bias_s,
        'sinks': p['attn_sinks'][l],
        'sink_s': jnp.broadcast_to(jnp.repeat(p['attn_sinks'][l], t_s)[:, None], (N_HEADS * t_s, LANES)),
        'wpa': p['w_proj_a'][l].astype(BF16), 'wpb': p['w_proj_b'][l].astype(BF16),
        'wpc': p['w_proj_c'][l].astype(BF16), 'wout': p['w_out'][l].astype(BF16),
    }


def _strict_lower(n):
    i = jnp.arange(n)
    return jnp.where(i[None, :] < i[:, None], 1.0, 0.0).astype(BF16)


def kernel(x_prompt, x_sample, cache_win_k, cache_win_v, state_conv, state_lru_h, norm1_g, w_in, chunk_ln_g, chunk_ln_b, chunk_ws, chunk_bs, conv_w, conv_b, lru_lambda, lru_wa, lru_ba, lru_wi, lru_bi, q_norm_g, k_norm_g, attn_sinks, w_proj_a, w_proj_b, w_proj_c, w_out, norm2_g, ffn_w_gate, ffn_w_up, ffn_w_down, moe_router, moe_w_gate, moe_w_up, moe_w_down):
    p = dict(norm1_g=norm1_g, w_in=w_in, chunk_ln_g=chunk_ln_g, chunk_ln_b=chunk_ln_b, chunk_ws=chunk_ws,
             chunk_bs=chunk_bs, conv_w=conv_w, conv_b=conv_b, lru_lambda=lru_lambda, lru_wa=lru_wa,
             lru_ba=lru_ba, lru_wi=lru_wi, lru_bi=lru_bi, q_norm_g=q_norm_g, k_norm_g=k_norm_g,
             attn_sinks=attn_sinks, w_proj_a=w_proj_a, w_proj_b=w_proj_b, w_proj_c=w_proj_c, w_out=w_out)
    depth = w_in.shape[0]
    bp, tp, _ = x_prompt.shape
    bs, ts, _ = x_sample.shape
    tb_p = min(1024, bp * tp)
    tb_s = min(1024, bs * ts)
    assert tb_p % MOE_SUB_TOKENS == 0 and tb_s % MOE_SUB_TOKENS == 0
    tri = _strict_lower(MOE_SUB_TOKENS)

    yp, ys = x_prompt, x_sample
    ck = cache_win_k.reshape(depth * bs, WINDOW, LANES)
    cv = cache_win_v.reshape(depth * bs, WINDOW, LANES)
    outs = {k: [] for k in ('pk', 'pv', 'pconv', 'ph', 'sk', 'sv', 'sconv', 'sh', 'schunk')}
    for l in range(depth):
        lw = _pack_layer(l, p, ts)
        yp, pk, pv, plx, phl = _mixer_prompt(yp, lw)
        cs = jnp.pad(state_conv[l], ((0, 0), (8 - (CONV_W - 1), 0), (0, 0)))
        ys, sk, sv, slx, shl, scv = _mixer_sample(ys, ck, cv, l, cs, state_lru_h[l][:, None, :], lw)
        outs['pk'].append(pk.reshape(bp, WINDOW, N_KV_HEADS, HEAD_DIM))
        outs['pv'].append(pv.reshape(bp, WINDOW, N_KV_HEADS, HEAD_DIM))
        outs['pconv'].append(plx[:, 8 - (CONV_W - 1):, :])
        outs['ph'].append(phl[:, 0, :])
        outs['sk'].append(sk.reshape(bs, WINDOW, N_KV_HEADS, HEAD_DIM))
        outs['sv'].append(sv.reshape(bs, WINDOW, N_KV_HEADS, HEAD_DIM))
        outs['sconv'].append(slx[:, ts - (CONV_W - 1):, :])
        outs['sh'].append(shl[:, ts - 1, :])
        outs['schunk'].append(scv)

        g2 = norm2_g[l][None]
        flat_p, flat_s = yp.reshape(bp * tp, D_MODEL), ys.reshape(bs * ts, D_MODEL)
        j = l // 2
        if l % 2 == 0:
            wg, wu, wd = ffn_w_gate[j].astype(BF16), ffn_w_up[j].astype(BF16), ffn_w_down[j].astype(BF16)
            flat_p = _ffn(flat_p, g2, wg, wu, wd)
            flat_s = _ffn(flat_s, g2, wg, wu, wd)
        else:
            wg, wu, wd = moe_w_gate[j].astype(BF16), moe_w_up[j].astype(BF16), moe_w_down[j].astype(BF16)
            wr = jnp.pad(moe_router[j], ((0, 0), (0, LANES - N_EXPERTS)))
            flat_p = _moe_ffn(flat_p, g2, wr, tri, wg, wu, wd, tb_p)
            flat_s = _moe_ffn(flat_s, g2, wr, tri, wg, wu, wd, tb_s)
        yp, ys = flat_p.reshape(bp, tp, D_MODEL), flat_s.reshape(bs, ts, D_MODEL)

    st = {k: jnp.stack(v) for k, v in outs.items()}
    return (yp, ys, st['pk'], st['pv'], st['pconv'], st['ph'],
            st['sk'], st['sv'], st['sconv'], st['sh'], st['schunk'])
```
